```python
import math
import jax, jax.numpy as jnp
from jax import lax
import numpy as np

D_MODEL = 1024
BATCH = 1
SEQ = 16384
DEPTH = 1

GRID_W = 64
MEM_LEN = 256
HEAD_DIM = 64
N_Q_HEADS = D_MODEL // HEAD_DIM
N_KV_HEADS = 4
GQA_GROUP = N_Q_HEADS // N_KV_HEADS
Q_WIDTH = N_Q_HEADS * HEAD_DIM
KV_WIDTH = N_KV_HEADS * HEAD_DIM
N_FOURIER_GROUPS = 4
FOURIER_GROUP_DIM = 128
FOURIER_WIDTH = N_FOURIER_GROUPS * FOURIER_GROUP_DIM
IN_WIDTH = Q_WIDTH + 2 * KV_WIDTH + FOURIER_WIDTH + 2 * D_MODEL
ROPE_AXIS_DIM = HEAD_DIM // 2
ROPE_THETA = 10000.0
Q_BLOCK = 128
N_CROSS_HEADS = 4
CROSS_HEAD_DIM = 64
CROSS_WIDTH = N_CROSS_HEADS * CROSS_HEAD_DIM
D_FF = 2816
RMS_EPS = 1e-6

kernel_name = "hybrid_fnet_axial_gqa_macaron_encoder"


def rmsnorm(x, g):
    xf = x.astype(jnp.float32)
    y = xf * lax.rsqrt(jnp.mean(xf * xf, axis=-1, keepdims=True) + RMS_EPS)
    return (y * g.astype(jnp.float32)).astype(x.dtype)


def swiglu(x, w_gate, w_up, w_down):
    return (jax.nn.silu(x @ w_gate) * (x @ w_up)) @ w_down


def axial_rope_tables(S, dtype):
    rows = S // GRID_W
    row = jnp.repeat(jnp.arange(rows, dtype=jnp.float32), GRID_W)
    col = jnp.tile(jnp.arange(GRID_W, dtype=jnp.float32), rows)
    inv = 1.0 / (ROPE_THETA ** (jnp.arange(0, ROPE_AXIS_DIM, 2, dtype=jnp.float32) / ROPE_AXIS_DIM))
    ang_r = row[:, None] * inv[None, :]
    ang_c = col[:, None] * inv[None, :]
    return (jnp.cos(ang_r).astype(dtype), jnp.sin(ang_r).astype(dtype),
            jnp.cos(ang_c).astype(dtype), jnp.sin(ang_c).astype(dtype))


def _rotate(x, cos, sin):
    S, F = cos.shape
    shape = (1, S) + (1,) * (x.ndim - 3) + (F,)
    c = cos.reshape(shape)
    s = sin.reshape(shape)
    x1, x2 = x[..., :F], x[..., F:]
    return jnp.concatenate([x1 * c - x2 * s, x2 * c + x1 * s], axis=-1)


def apply_axial_rope(x, tables):
    cr, sr, cc, sc = tables
    xr = _rotate(x[..., :ROPE_AXIS_DIM], cr, sr)
    xc = _rotate(x[..., ROPE_AXIS_DIM:], cc, sc)
    return jnp.concatenate([xr, xc], axis=-1)


def fourier_mixer(f):
    B, S, _ = f.shape
    fg = f.reshape(B, S, N_FOURIER_GROUPS, FOURIER_GROUP_DIM).astype(jnp.float32)
    y = jnp.real(jnp.fft.fft2(fg, axes=(1, 3), norm="ortho"))
    return y.reshape(B, S, FOURIER_WIDTH).astype(f.dtype)


def gqa_axial_attention(q, k, v, q_gain, k_gain, tables):
    B, S, _ = q.shape
    q = rmsnorm(q.reshape(B, S, N_KV_HEADS, GQA_GROUP, HEAD_DIM), q_gain)
    k = rmsnorm(k.reshape(B, S, N_KV_HEADS, HEAD_DIM), k_gain)
    v = v.reshape(B, S, N_KV_HEADS, HEAD_DIM)
    q = apply_axial_rope(q, tables)
    k = apply_axial_rope(k, tables)
    scale = 1.0 / math.sqrt(HEAD_DIM)
    nb = S // Q_BLOCK
    qb = q.reshape(B, nb, Q_BLOCK, N_KV_HEADS, GQA_GROUP, HEAD_DIM).transpose(1, 0, 2, 3, 4, 5)

    def attend(qblk):
        s = jnp.einsum('bqkgd,bskd->bkgqs', qblk, k).astype(jnp.float32) * scale
        p = jax.nn.softmax(s, axis=-1).astype(v.dtype)
        return jnp.einsum('bkgqs,bskd->bqkgd', p, v)

    o = lax.map(attend, qb)
    return o.transpose(1, 0, 2, 3, 4, 5).reshape(B, S, Q_WIDTH)


def memory_cross_attention(h_n, mem_n, w_cq, w_ck, w_cv, w_co):
    B, S, _ = h_n.shape
    M = mem_n.shape[1]
    q = (h_n @ w_cq).reshape(B, S, N_CROSS_HEADS, CROSS_HEAD_DIM)
    k = (mem_n @ w_ck).reshape(B, M, N_CROSS_HEADS, CROSS_HEAD_DIM)
    v = (mem_n @ w_cv).reshape(B, M, N_CROSS_HEADS, CROSS_HEAD_DIM)
    s = jnp.einsum('bshd,bmhd->bhsm', q, k).astype(jnp.float32) * (1.0 / math.sqrt(CROSS_HEAD_DIM))
    p = jax.nn.softmax(s, axis=-1).astype(v.dtype)
    o = jnp.einsum('bhsm,bmhd->bshd', p, v).reshape(B, S, CROSS_WIDTH)
    return o @ w_co


def setup_inputs(seed: int = 0) -> dict:
    key = jax.random.key(seed)
    ks = iter(jax.random.split(key, 32))
    f32 = jnp.float32

    def dense(shape):
        return jax.random.normal(next(ks), (DEPTH,) + shape, f32) * (shape[0] ** -0.5)

    def gain(shape):
        return 1.0 + 0.1 * jax.random.normal(next(ks), shape, f32)

    return {
        "x": jax.random.normal(next(ks), (BATCH, SEQ, D_MODEL), f32),
        "mem": jax.random.normal(next(ks), (BATCH, MEM_LEN, D_MODEL), f32),
        "g_ffn1": gain((DEPTH, D_MODEL)),
        "w1_gate": dense((D_MODEL, D_FF)),
        "w1_up": dense((D_MODEL, D_FF)),
        "w1_down": dense((D_FF, D_MODEL)),
        "g_mix": gain((DEPTH, D_MODEL)),
        "w_in": dense((D_MODEL, IN_WIDTH)),
        "q_gain": gain((DEPTH, HEAD_DIM)),
        "k_gain": gain((DEPTH, HEAD_DIM)),
        "w_fourier_branch": dense((FOURIER_WIDTH, D_MODEL)),
        "w_attn_branch": dense((Q_WIDTH, D_MODEL)),
        "w_out": dense((D_MODEL, D_MODEL)),
        "g_cross": gain((DEPTH, D_MODEL)),
        "g_mem": gain((DEPTH, D_MODEL)),
        "w_cq": dense((D_MODEL, CROSS_WIDTH)),
        "w_ck": dense((D_MODEL, CROSS_WIDTH)),
        "w_cv": dense((D_MODEL, CROSS_WIDTH)),
        "w_co": dense((CROSS_WIDTH, D_MODEL)),
        "g_ffn2": gain((DEPTH, D_MODEL)),
        "w2_gate": dense((D_MODEL, D_FF)),
        "w2_up": dense((D_MODEL, D_FF)),
        "w2_down": dense((D_FF, D_MODEL)),
        "g_final": gain((D_MODEL,)),
    }


def reference(x, mem, g_ffn1, w1_gate, w1_up, w1_down, g_mix, w_in, q_gain, k_gain,
              w_fourier_branch, w_attn_branch, w_out, g_cross, g_mem, w_cq, w_ck, w_cv,
              w_co, g_ffn2, w2_gate, w2_up, w2_down, g_final):
    S = x.shape[1]
    tables = axial_rope_tables(S, x.dtype)
    split_at = [Q_WIDTH, Q_WIDTH + KV_WIDTH, Q_WIDTH + 2 * KV_WIDTH,
                Q_WIDTH + 2 * KV_WIDTH + FOURIER_WIDTH,
                Q_WIDTH + 2 * KV_WIDTH + FOURIER_WIDTH + D_MODEL]
    h = x
    for l in range(DEPTH):
        h = h + 0.5 * swiglu(rmsnorm(h, g_ffn1[l]), w1_gate[l], w1_up[l], w1_down[l])

        n = rmsnorm(h, g_mix[l])
        p = n @ w_in[l]
        q, k, v, f, gate_f, gate_a = jnp.split(p, split_at, axis=-1)
        y_f = fourier_mixer(f) @ w_fourier_branch[l]
        y_a = gqa_axial_attention(q, k, v, q_gain[l], k_gain[l], tables) @ w_attn_branch[l]
        merged = jax.nn.sigmoid(gate_f) * y_f + jax.nn.sigmoid(gate_a) * y_a
        h = h + merged @ w_out[l]

        h = h + memory_cross_attention(rmsnorm(h, g_cross[l]), rmsnorm(mem, g_mem[l]),
                                       w_cq[l], w_ck[l], w_cv[l], w_co[l])

        h = h + 0.5 * swiglu(rmsnorm(h, g_ffn2[l]), w2_gate[l], w2_up[l], w2_down[l])
    return rmsnorm(h, g_final)
```

```python
import functools
import math

import numpy as np
import jax
import jax.numpy as jnp
from jax import lax
from jax.experimental import pallas as pl
from jax.experimental.pallas import tpu as pltpu

F32 = jnp.float32
BF16 = jnp.bfloat16

LANES = 128
GRID_W = 64
HEAD_DIM = 64
N_KV_HEADS = 4
GQA_GROUP = 4
N_CROSS_HEADS = 4
FOURIER_GROUPS = 4
FOURIER_GROUP_DIM = 128
ROPE_AXIS_DIM = HEAD_DIM // 2
ROPE_HALF = ROPE_AXIS_DIM // 2
ROPE_THETA = 10000.0
RMS_EPS = 1e-6
FFT_INNER = 128
VMEM_LIMIT = 56 * 1024 * 1024


def _cparams(n_axes):
    return pltpu.CompilerParams(dimension_semantics=("arbitrary",) * n_axes,
                                vmem_limit_bytes=VMEM_LIMIT)


def _resident(shape):
    zeros = (0,) * len(shape)
    return pl.BlockSpec(shape, lambda *_: zeros, pipeline_mode=pl.Buffered(1))


def _dot(a, b):
    return jnp.dot(a, b, preferred_element_type=F32)


def _rms(x, g):
    y = x * lax.rsqrt(jnp.mean(x * x, axis=-1, keepdims=True) + RMS_EPS)
    return y * g


def _ffn_kernel(x_ref, g_ref, wg_ref, wu_ref, wd_ref, gfin_ref, o_ref, *, final_norm):
    x = x_ref[...]
    xn = _rms(x, g_ref[...]).astype(BF16)
    gate = _dot(xn, wg_ref[...])
    up = _dot(xn, wu_ref[...])
    act = (gate * jax.nn.sigmoid(gate) * up).astype(BF16)
    h = x + 0.5 * _dot(act, wd_ref[...])
    if final_norm:
        h = _rms(h, gfin_ref[...])
    o_ref[...] = h


def _ffn(x, g, wg, wu, wd, gfin, *, final_norm, tm):
    s, d = x.shape
    dff = wg.shape[1]
    row = pl.BlockSpec((tm, d), lambda i: (i, 0))
    return pl.pallas_call(
        functools.partial(_ffn_kernel, final_norm=final_norm),
        grid=(s // tm,),
        in_specs=[row, _resident((1, d)), _resident((d, dff)), _resident((d, dff)),
                  _resident((dff, d)), _resident((1, d))],
        out_specs=row,
        out_shape=jax.ShapeDtypeStruct((s, d), F32),
        compiler_params=_cparams(1),
        name="ffn_final" if final_norm else "ffn",
    )(x, g, wg, wu, wd, gfin)


def _head_norm_rope(xc, gain, head_mean, cos, sin, first_half):
    ms = _dot((xc * xc).astype(BF16), head_mean)
    y = xc * lax.rsqrt(ms + RMS_EPS) * gain
    partner = jnp.where(first_half,
                        pltpu.roll(y, LANES - ROPE_HALF, 1),
                        pltpu.roll(y, ROPE_HALF, 1))
    return y * cos + partner * sin


def _inproj_kernel(h_ref, g_ref, win_ref, qg_ref, kg_ref, cos_ref, sin_ref, hm_ref, vone_ref,
                   q_ref, kt_ref, v_ref, f_ref, sf_ref, sa_ref, *, widths):
    qw, kw, vw, fw, gw = widths
    n = _rms(h_ref[...], g_ref[...]).astype(BF16)
    cos = cos_ref[...]
    sin = sin_ref[...]
    head_mean = hm_ref[...]
    lane = lax.broadcasted_iota(jnp.int32, cos.shape, 1)
    first_half = (lane % ROPE_AXIS_DIM) < ROPE_HALF

    def proj(lo, width):
        return _dot(n, win_ref[:, lo:lo + width])

    q = proj(0, qw)
    for c in range(qw // LANES):
        sl = slice(c * LANES, (c + 1) * LANES)
        qc = _head_norm_rope(q[:, sl], qg_ref[...], head_mean, cos, sin, first_half)
        q_ref[:, sl] = (qc * (1.0 / math.sqrt(HEAD_DIM))).astype(BF16)
    k = proj(qw, kw)
    k_rot = jnp.concatenate(
        [_head_norm_rope(k[:, c * LANES:(c + 1) * LANES], kg_ref[...], head_mean, cos, sin, first_half)
         for c in range(kw // LANES)], axis=1)
    kt_ref[...] = k_rot.T.astype(BF16)
    lo = qw + kw
    v_ref[...] = (proj(lo, vw) + vone_ref[...]).astype(BF16)
    lo += vw
    f_ref[...] = proj(lo, fw).astype(BF16)
    lo += fw
    sf_ref[...] = jax.nn.sigmoid(proj(lo, gw))
    lo += gw
    sa_ref[...] = jax.nn.sigmoid(proj(lo, gw))


def _inproj(h, g, win, qg, kg, cos, sin, head_mean, vone, *, widths, tm):
    s, d = h.shape
    qw, kw, vw, fw, gw = widths

    def rows(w):
        return pl.BlockSpec((tm, w), lambda i: (i, 0))

    return pl.pallas_call(
        functools.partial(_inproj_kernel, widths=widths),
        grid=(s // tm,),
        in_specs=[rows(d), _resident((1, d)), _resident(win.shape), _resident((1, LANES)),
                  _resident((1, LANES)), rows(LANES), rows(LANES), _resident((LANES, LANES)),
                  _resident((1, vw))],
        out_specs=[rows(qw), pl.BlockSpec((kw, tm), lambda i: (0, i)), rows(vw), rows(fw),
                   rows(gw), rows(gw)],
        out_shape=[jax.ShapeDtypeStruct((s, qw), BF16), jax.ShapeDtypeStruct((kw, s), BF16),
                   jax.ShapeDtypeStruct((s, vw), BF16), jax.ShapeDtypeStruct((s, fw), BF16),
                   jax.ShapeDtypeStruct((s, gw), F32), jax.ShapeDtypeStruct((s, gw), F32)],
        compiler_params=_cparams(1),
        name="inproj",
    )(h, g, win, qg, kg, cos, sin, head_mean, vone)


def _fft1_kernel(x_ref, cs1_ref, a_ref):
    a_ref[...] = _dot(cs1_ref[...], x_ref[...])


def _fft1(x2d, cs1, *, tn):
    n1, cols = x2d.shape
    return pl.pallas_call(
        _fft1_kernel,
        grid=(cols // tn,),
        in_specs=[pl.BlockSpec((n1, tn), lambda j: (0, j)), _resident((2 * n1, n1))],
        out_specs=pl.BlockSpec((2 * n1, tn), lambda j: (0, j)),
        out_shape=jax.ShapeDtypeStruct((2 * n1, cols), F32),
        compiler_params=_cparams(1),
        name="fft1",
    )(x2d, cs1)


def _fft2_kernel(ac_ref, as_ref, tc_ref, ts_ref, cs2_ref, mix_ref, o_ref):
    kb, n2, width = ac_ref.shape
    reps = width // LANES
    for kk in range(kb):
        ac = ac_ref[kk]
        as_ = as_ref[kk]
        tc = jnp.concatenate([tc_ref[kk]] * reps, axis=1)
        ts = jnp.concatenate([ts_ref[kk]] * reps, axis=1)
        b = jnp.concatenate([ac * tc - as_ * ts, as_ * tc + ac * ts], axis=1).astype(BF16)
        r = _dot(cs2_ref[...], b)
        pc = r[:n2, :width] - r[n2:, width:]
        ps = r[:n2, width:] + r[n2:, :width]
        pp = jnp.concatenate([pc, ps], axis=1).astype(BF16)
        o_ref[:, kk * width:(kk + 1) * width] = _dot(pp, mix_ref[...]).astype(BF16)


def _fft2(a3, tc, ts, cs2, mix, *, kb):
    two_n1, n2, width = a3.shape
    n1 = two_n1 // 2
    nblk = n1 // kb
    return pl.pallas_call(
        _fft2_kernel,
        grid=(nblk,),
        in_specs=[pl.BlockSpec((kb, n2, width), lambda j: (j, 0, 0)),
                  pl.BlockSpec((kb, n2, width), lambda j: (j + nblk, 0, 0)),
                  pl.BlockSpec((kb, n2, LANES), lambda j: (j, 0, 0)),
                  pl.BlockSpec((kb, n2, LANES), lambda j: (j, 0, 0)),
                  _resident(cs2.shape), _resident(mix.shape)],
        out_specs=pl.BlockSpec((n2, kb * width), lambda j: (0, j)),
        out_shape=jax.ShapeDtypeStruct((n2, n1 * width), BF16),
        compiler_params=_cparams(1),
        name="fft2",
    )(a3, a3, tc, ts, cs2, mix)


def _dft_tables(s):
    n1 = s // FFT_INNER

    def cos_sin(n, scale):
        ang = 2.0 * np.pi * (np.outer(np.arange(n), np.arange(n)) % n) / n
        return np.cos(ang) * scale, np.sin(ang) * scale

    c1, s1 = cos_sin(n1, n1 ** -0.5)
    c2, s2 = cos_sin(FFT_INNER, FFT_INNER ** -0.5)
    cc, sc = cos_sin(FOURIER_GROUP_DIM, FOURIER_GROUP_DIM ** -0.5)
    eye = np.eye(FOURIER_GROUPS)
    mix = np.concatenate([np.kron(eye, cc), -np.kron(eye, sc)], axis=0)
    as_bf16 = lambda a: jnp.asarray(a, dtype=F32).astype(BF16)
    return (as_bf16(np.concatenate([c1, s1], axis=0)), as_bf16(np.concatenate([c2, s2], axis=0)),
            as_bf16(mix))


def _twiddles(s):
    n1 = s // FFT_INNER
    prod = (jnp.arange(n1, dtype=jnp.int32)[:, None] * jnp.arange(FFT_INNER, dtype=jnp.int32)[None, :]) % s
    ang = prod.astype(F32) * (2.0 * math.pi / s)
    shape = (n1, FFT_INNER, LANES)
    return (jnp.broadcast_to(jnp.cos(ang)[:, :, None], shape),
            jnp.broadcast_to(jnp.sin(ang)[:, :, None], shape))


def _attn_kernel(q_ref, kt_ref, v_ref, o_ref, *, tk):
    tq = q_ref.shape[0]
    q = q_ref[...]
    qs = jnp.concatenate([q[:, g * HEAD_DIM:(g + 1) * HEAD_DIM] for g in range(GQA_GROUP)], axis=0)
    rows = GQA_GROUP * tq

    def body(j, carry):
        m, acc = carry
        off = pl.multiple_of(j * tk, tk)
        s = _dot(qs, kt_ref[:, pl.ds(off, tk)])
        m_new = jnp.maximum(m, jnp.max(s, axis=1, keepdims=True))
        p = jnp.exp(s - m_new).astype(BF16)
        acc = acc * jnp.exp(m - m_new) + _dot(p, v_ref[pl.ds(off, tk), :])
        return m_new, acc

    m0 = jnp.full((rows, 1), -jnp.inf, F32)
    acc0 = jnp.zeros((rows, LANES), F32)
    _, acc = lax.fori_loop(0, kt_ref.shape[1] // tk, body, (m0, acc0))
    out = acc[:, :HEAD_DIM] / acc[:, HEAD_DIM:HEAD_DIM + 1]
    o_ref[...] = jnp.concatenate([out[g * tq:(g + 1) * tq] for g in range(GQA_GROUP)],
                                 axis=1).astype(BF16)


def _attn(q, kt, v, *, tq, tk):
    s, qw = q.shape
    gw = GQA_GROUP * HEAD_DIM
    return pl.pallas_call(
        functools.partial(_attn_kernel, tk=tk),
        grid=(N_KV_HEADS, s // tq),
        in_specs=[pl.BlockSpec((tq, gw), lambda h, i: (i, h)),
                  pl.BlockSpec((HEAD_DIM, s), lambda h, i: (h, 0)),
                  pl.BlockSpec((s, LANES), lambda h, i: (0, h))],
        out_specs=pl.BlockSpec((tq, gw), lambda h, i: (i, h)),
        out_shape=jax.ShapeDtypeStruct((s, qw), BF16),
        compiler_params=_cparams(2),
        name="attn",
    )(q, kt, v)


def _memkv_kernel(mem_ref, g_ref, wk_ref, wv_ref, ktm_ref, vm_ref):
    mn = _rms(mem_ref[...], g_ref[...]).astype(BF16)
    kt = _dot(mn, wk_ref[...]).T
    v = _dot(mn, wv_ref[...])
    feat_row = lax.broadcasted_iota(jnp.int32, kt.shape, 0) // HEAD_DIM
    feat_col = lax.broadcasted_iota(jnp.int32, v.shape, 1) // HEAD_DIM
    for h in range(N_CROSS_HEADS):
        ktm_ref[h] = jnp.where(feat_row == h, kt, 0.0).astype(BF16)
        vm_ref[h] = jnp.where(feat_col == h, v, 0.0).astype(BF16)


def _memkv(mem, g, wk, wv):
    m, d = mem.shape
    cw = wk.shape[1]
    whole = lambda shape: pl.BlockSpec(shape, lambda: (0,) * len(shape))
    return pl.pallas_call(
        _memkv_kernel,
        in_specs=[whole((m, d)), whole((1, d)), whole((d, cw)), whole((d, cw))],
        out_specs=[whole((N_CROSS_HEADS, cw, m)), whole((N_CROSS_HEADS, m, cw))],
        out_shape=[jax.ShapeDtypeStruct((N_CROSS_HEADS, cw, m), BF16),
                   jax.ShapeDtypeStruct((N_CROSS_HEADS, m, cw), BF16)],
        compiler_params=pltpu.CompilerParams(vmem_limit_bytes=VMEM_LIMIT),
        name="memkv",
    )(mem, g, wk, wv)


def _merge_kernel(h_ref, fm_ref, o_ref, sf_ref, sa_ref, wfb_ref, wab_ref, wout_ref, gc_ref,
                  wcq_ref, ktm_ref, vm_ref, wco_ref, out_ref):
    y_f = _dot(fm_ref[...], wfb_ref[...])
    y_a = _dot(o_ref[...], wab_ref[...])
    merged = (sf_ref[...] * y_f + sa_ref[...] * y_a).astype(BF16)
    h = h_ref[...] + _dot(merged, wout_ref[...])
    n = _rms(h, gc_ref[...]).astype(BF16)
    qc = (_dot(n, wcq_ref[...]) * (1.0 / math.sqrt(HEAD_DIM))).astype(BF16)
    oc = None
    for hh in range(N_CROSS_HEADS):
        s = _dot(qc, ktm_ref[hh])
        e = jnp.exp(s - jnp.max(s, axis=1, keepdims=True))
        p = (e / jnp.sum(e, axis=1, keepdims=True)).astype(BF16)
        head = _dot(p, vm_ref[hh])
        oc = head if oc is None else oc + head
    out_ref[...] = h + _dot(oc.astype(BF16), wco_ref[...])


def _merge(h, fm, o, sf, sa, wfb, wab, wout, gc, wcq, ktm, vm, wco, *, tm):
    s, d = h.shape

    def rows(w):
        return pl.BlockSpec((tm, w), lambda i: (i, 0))

    weights = [wfb, wab, wout, gc, wcq, ktm, vm, wco]
    return pl.pallas_call(
        _merge_kernel,
        grid=(s // tm,),
        in_specs=[rows(d), rows(fm.shape[1]), rows(o.shape[1]), rows(d), rows(d)]
                 + [_resident(w.shape) for w in weights],
        out_specs=rows(d),
        out_shape=jax.ShapeDtypeStruct((s, d), F32),
        compiler_params=_cparams(1),
        name="merge",
    )(h, fm, o, sf, sa, *weights)


def _rope_tables(s):
    t = jnp.arange(s, dtype=jnp.int32)
    row = (t // GRID_W).astype(F32)
    col = (t % GRID_W).astype(F32)
    inv = 1.0 / (ROPE_THETA ** (jnp.arange(0, ROPE_AXIS_DIM, 2, dtype=F32) / ROPE_AXIS_DIM))
    ang_r = row[:, None] * inv[None, :]
    ang_c = col[:, None] * inv[None, :]
    ang = jnp.concatenate([ang_r, ang_r, ang_c, ang_c] * (LANES // HEAD_DIM), axis=1)
    sign = np.where((np.arange(LANES) % ROPE_AXIS_DIM) < ROPE_HALF, -1.0, 1.0).astype(np.float32)
    return jnp.cos(ang), jnp.sin(ang) * sign[None, :]


def _pick_tile(s, want):
    t = min(want, s)
    while s % t:
        t //= 2
    return t


def kernel(x, mem, g_ffn1, w1_gate, w1_up, w1_down, g_mix, w_in, q_gain, k_gain, w_fourier_branch, w_attn_branch, w_out, g_cross, g_mem, w_cq, w_ck, w_cv, w_co, g_ffn2, w2_gate, w2_up, w2_down, g_final):
    batch, s, d = x.shape
    depth = w_in.shape[0]
    qw = w_attn_branch.shape[1]
    kw = N_KV_HEADS * HEAD_DIM
    fw = w_fourier_branch.shape[1]
    vw = N_KV_HEADS * LANES
    assert s % (FFT_INNER * 8) == 0 and s % GRID_W == 0
    assert qw == N_KV_HEADS * GQA_GROUP * HEAD_DIM and fw == FOURIER_GROUPS * FOURIER_GROUP_DIM
    n1 = s // FFT_INNER
    tm = _pick_tile(s, 512)
    row = lambda g: g.reshape(1, -1).astype(F32)
    bf = lambda w: w.astype(BF16)

    cos, sin = _rope_tables(s)
    cs1, cs2, mix = _dft_tables(s)
    tw_c, tw_s = _twiddles(s)
    lane = np.arange(LANES)
    head_mean = jnp.asarray((lane[:, None] // HEAD_DIM == lane[None, :] // HEAD_DIM) / HEAD_DIM,
                            dtype=F32).astype(BF16)
    vone = jnp.asarray((np.arange(vw) % LANES == HEAD_DIM).astype(np.float32)).reshape(1, vw)

    outs = []
    for b in range(batch):
        h = x[b]
        for l in range(depth):
            h = _ffn(h, row(g_ffn1[l]), bf(w1_gate[l]), bf(w1_up[l]), bf(w1_down[l]),
                     row(g_final), final_norm=False, tm=tm)

            w = w_in[l]
            wv = w[:, qw + kw:qw + 2 * kw].reshape(d, N_KV_HEADS, HEAD_DIM)
            wv = jnp.pad(wv, ((0, 0), (0, 0), (0, LANES - HEAD_DIM))).reshape(d, vw)
            win = bf(jnp.concatenate([w[:, :qw + kw], wv, w[:, qw + 2 * kw:]], axis=1))
            q, kt, v, f, sf, sa = _inproj(
                h, row(g_mix[l]), win, row(jnp.tile(q_gain[l], LANES // HEAD_DIM)),
                row(jnp.tile(k_gain[l], LANES // HEAD_DIM)), cos, sin, head_mean, vone,
                widths=(qw, kw, vw, fw, d), tm=tm)

            a = _fft1(f.reshape(n1, FFT_INNER * fw), cs1, tn=_pick_tile(FFT_INNER * fw, 8192))
            fm = _fft2(a.reshape(2 * n1, FFT_INNER, fw), tw_c, tw_s, cs2, mix, kb=8)
            fm = fm.reshape(s, fw)

            o = _attn(q, kt, v, tq=_pick_tile(s, 128), tk=_pick_tile(s, 512))

            ktm, vm = _memkv(mem[b], row(g_mem[l]), bf(w_ck[l]), bf(w_cv[l]))
            h = _merge(h, fm, o, sf, sa, bf(w_fourier_branch[l]), bf(w_attn_branch[l]), bf(w_out[l]),
                       row(g_cross[l]), bf(w_cq[l]), ktm, vm, bf(w_co[l]), tm=tm)

            h = _ffn(h, row(g_ffn2[l]), bf(w2_gate[l]), bf(w2_up[l]), bf(w2_down[l]),
                     row(g_final), final_norm=(l == depth - 1), tm=tm)
        outs.append(h)
    return jnp.stack(outs, axis=0)
```

```python
import functools
import math

import numpy as np
import jax
import jax.numpy as jnp
from jax import lax
from jax.experimental import pallas as pl
from jax.experimental.pallas import tpu as pltpu

F32 = jnp.float32
BF16 = jnp.bfloat16

LANES = 128
GRID_W = 64
HEAD_DIM = 64
N_KV_HEADS = 4
GQA_GROUP = 4
N_CROSS_HEADS = 4
FOURIER_GROUPS = 4
FOURIER_GROUP_DIM = 128
ROPE_AXIS_DIM = HEAD_DIM // 2
ROPE_HALF = ROPE_AXIS_DIM // 2
ROPE_THETA = 10000.0
RMS_EPS = 1e-6
FFT_INNER = 128
VMEM_LIMIT = 56 * 1024 * 1024
LOG2E = math.log2(math.e)
MAX_SAFE_SHIFT = 50.0


def _cparams(n_axes):
    return pltpu.CompilerParams(dimension_semantics=("arbitrary",) * n_axes,
                                vmem_limit_bytes=VMEM_LIMIT)


def _resident(shape):
    zeros = (0,) * len(shape)
    return pl.BlockSpec(shape, lambda *_: zeros, pipeline_mode=pl.Buffered(1))


def _dot(a, b):
    return jnp.dot(a, b, preferred_element_type=F32)


def _rms(x, g):
    y = x * lax.rsqrt(jnp.mean(x * x, axis=-1, keepdims=True) + RMS_EPS)
    return y * g


def _ffn_kernel(x_ref, g_ref, wg_ref, wu_ref, wd_ref, gfin_ref, o_ref, *, final_norm):
    x = x_ref[...]
    xn = _rms(x, g_ref[...]).astype(BF16)
    gate = _dot(xn, wg_ref[...])
    up = _dot(xn, wu_ref[...])
    act = (gate * jax.nn.sigmoid(gate) * up).astype(BF16)
    h = x + 0.5 * _dot(act, wd_ref[...])
    if final_norm:
        h = _rms(h, gfin_ref[...])
    o_ref[...] = h


def _ffn(x, g, wg, wu, wd, gfin, *, final_norm, tm):
    s, d = x.shape
    dff = wg.shape[1]
    row = pl.BlockSpec((tm, d), lambda i: (i, 0))
    return pl.pallas_call(
        functools.partial(_ffn_kernel, final_norm=final_norm),
        grid=(s // tm,),
        in_specs=[row, _resident((1, d)), _resident((d, dff)), _resident((d, dff)),
                  _resident((dff, d)), _resident((1, d))],
        out_specs=row,
        out_shape=jax.ShapeDtypeStruct((s, d), F32),
        compiler_params=_cparams(1),
        name="ffn_final" if final_norm else "ffn",
    )(x, g, wg, wu, wd, gfin)


def _head_norm_rope(xc, gain, head_mean, cos, sin, first_half):
    ms = _dot((xc * xc).astype(BF16), head_mean)
    y = xc * lax.rsqrt(ms + RMS_EPS) * gain
    partner = jnp.where(first_half,
                        pltpu.roll(y, LANES - ROPE_HALF, 1),
                        pltpu.roll(y, ROPE_HALF, 1))
    return y * cos + partner * sin


def _head_slots(xc, extra, low):
    return jnp.where(low, xc, extra), jnp.where(low, pltpu.roll(xc, HEAD_DIM, 1), extra)


def _inproj_kernel(h_ref, g_ref, win_ref, qg_ref, kg_ref, cos_ref, sin_ref, hm_ref, qx_ref, kx_ref,
                   vone_ref, q_ref, kt_ref, v_ref, f_ref, sf_ref, sa_ref, *, widths):
    qw, kw, vw, fw, gw = widths
    n = _rms(h_ref[...], g_ref[...]).astype(BF16)
    cos = cos_ref[...]
    sin = sin_ref[...]
    head_mean = hm_ref[...]
    lane = lax.broadcasted_iota(jnp.int32, cos.shape, 1)
    first_half = (lane % ROPE_AXIS_DIM) < ROPE_HALF
    low = lane < HEAD_DIM

    def proj(lo, width):
        return _dot(n, win_ref[:, lo:lo + width])

    q = proj(0, qw)
    for c in range(qw // LANES):
        qc = _head_norm_rope(q[:, c * LANES:(c + 1) * LANES], qg_ref[...], head_mean, cos, sin,
                             first_half) * (LOG2E / math.sqrt(HEAD_DIM))
        for half, slot in enumerate(_head_slots(qc, qx_ref[...], low)):
            lo = (2 * c + half) * LANES
            q_ref[:, lo:lo + LANES] = slot.astype(BF16)
    k = proj(qw, kw)
    k_slots = []
    for c in range(kw // LANES):
        kc = _head_norm_rope(k[:, c * LANES:(c + 1) * LANES], kg_ref[...], head_mean, cos, sin,
                             first_half)
        k_slots.extend(_head_slots(kc, kx_ref[...], low))
    kt_ref[...] = jnp.concatenate(k_slots, axis=1).T.astype(BF16)
    lo = qw + kw
    v_ref[...] = (proj(lo, vw) + vone_ref[...]).astype(BF16)
    lo += vw
    f_ref[...] = proj(lo, fw).astype(BF16)
    lo += fw
    sf_ref[...] = jax.nn.sigmoid(proj(lo, gw))
    lo += gw
    sa_ref[...] = jax.nn.sigmoid(proj(lo, gw))


def _inproj(h, g, win, qg, kg, cos, sin, head_mean, qx, kx, vone, *, widths, tm):
    s, d = h.shape
    qw, kw, vw, fw, gw = widths
    slots = LANES // HEAD_DIM

    def rows(w):
        return pl.BlockSpec((tm, w), lambda i: (i, 0))

    lane_row = _resident((1, LANES))
    return pl.pallas_call(
        functools.partial(_inproj_kernel, widths=widths),
        grid=(s // tm,),
        in_specs=[rows(d), _resident((1, d)), _resident(win.shape), lane_row, lane_row,
                  rows(LANES), rows(LANES), _resident((LANES, LANES)), lane_row, lane_row,
                  _resident((1, vw))],
        out_specs=[rows(qw * slots), pl.BlockSpec((kw * slots, tm), lambda i: (0, i)), rows(vw),
                   rows(fw), rows(gw), rows(gw)],
        out_shape=[jax.ShapeDtypeStruct((s, qw * slots), BF16),
                   jax.ShapeDtypeStruct((kw * slots, s), BF16),
                   jax.ShapeDtypeStruct((s, vw), BF16), jax.ShapeDtypeStruct((s, fw), BF16),
                   jax.ShapeDtypeStruct((s, gw), F32), jax.ShapeDtypeStruct((s, gw), F32)],
        compiler_params=_cparams(1),
        name="inproj",
    )(h, g, win, qg, kg, cos, sin, head_mean, qx, kx, vone)


def _fft1_kernel(x_ref, cs1_ref, a_ref):
    a_ref[...] = _dot(cs1_ref[...], x_ref[...])


def _fft1(x2d, cs1, *, tn):
    n1, cols = x2d.shape
    return pl.pallas_call(
        _fft1_kernel,
        grid=(cols // tn,),
        in_specs=[pl.BlockSpec((n1, tn), lambda j: (0, j)), _resident((2 * n1, n1))],
        out_specs=pl.BlockSpec((2 * n1, tn), lambda j: (0, j)),
        out_shape=jax.ShapeDtypeStruct((2 * n1, cols), F32),
        compiler_params=_cparams(1),
        name="fft1",
    )(x2d, cs1)


def _fft2_kernel(ac_ref, as_ref, tc_ref, ts_ref, cs2_ref, mix_ref, o_ref):
    kb, n2, width = ac_ref.shape
    reps = width // LANES
    for kk in range(kb):
        ac = ac_ref[kk]
        as_ = as_ref[kk]
        tc = jnp.concatenate([tc_ref[kk]] * reps, axis=1)
        ts = jnp.concatenate([ts_ref[kk]] * reps, axis=1)
        b = jnp.concatenate([ac * tc - as_ * ts, as_ * tc + ac * ts], axis=1).astype(BF16)
        r = _dot(cs2_ref[...], b)
        pc = r[:n2, :width] - r[n2:, width:]
        ps = r[:n2, width:] + r[n2:, :width]
        pp = jnp.concatenate([pc, ps], axis=1).astype(BF16)
        o_ref[:, kk * width:(kk + 1) * width] = _dot(pp, mix_ref[...]).astype(BF16)


def _fft2(a3, tc, ts, cs2, mix, *, kb):
    two_n1, n2, width = a3.shape
    n1 = two_n1 // 2
    nblk = n1 // kb
    return pl.pallas_call(
        _fft2_kernel,
        grid=(nblk,),
        in_specs=[pl.BlockSpec((kb, n2, width), lambda j: (j, 0, 0)),
                  pl.BlockSpec((kb, n2, width), lambda j: (j + nblk, 0, 0)),
                  pl.BlockSpec((kb, n2, LANES), lambda j: (j, 0, 0)),
                  pl.BlockSpec((kb, n2, LANES), lambda j: (j, 0, 0)),
                  _resident(cs2.shape), _resident(mix.shape)],
        out_specs=pl.BlockSpec((n2, kb * width), lambda j: (0, j)),
        out_shape=jax.ShapeDtypeStruct((n2, n1 * width), BF16),
        compiler_params=_cparams(1),
        name="fft2",
    )(a3, a3, tc, ts, cs2, mix)


def _dft_tables(s):
    n1 = s // FFT_INNER

    def cos_sin(n, scale):
        ang = 2.0 * np.pi * (np.outer(np.arange(n), np.arange(n)) % n) / n
        return np.cos(ang) * scale, np.sin(ang) * scale

    c1, s1 = cos_sin(n1, n1 ** -0.5)
    c2, s2 = cos_sin(FFT_INNER, FFT_INNER ** -0.5)
    cc, sc = cos_sin(FOURIER_GROUP_DIM, FOURIER_GROUP_DIM ** -0.5)
    eye = np.eye(FOURIER_GROUPS)
    mix = np.concatenate([np.kron(eye, cc), -np.kron(eye, sc)], axis=0)
    as_bf16 = lambda a: jnp.asarray(a, dtype=F32).astype(BF16)
    return (as_bf16(np.concatenate([c1, s1], axis=0)), as_bf16(np.concatenate([c2, s2], axis=0)),
            as_bf16(mix))


def _twiddles(s):
    n1 = s // FFT_INNER
    prod = (jnp.arange(n1, dtype=jnp.int32)[:, None] * jnp.arange(FFT_INNER, dtype=jnp.int32)[None, :]) % s
    ang = prod.astype(F32) * (2.0 * math.pi / s)
    shape = (n1, FFT_INNER, LANES)
    return (jnp.broadcast_to(jnp.cos(ang)[:, :, None], shape),
            jnp.broadcast_to(jnp.sin(ang)[:, :, None], shape))


def _attn_kernel(q_ref, kt_ref, v_ref, o_ref, *, tk, online):
    tq = q_ref.shape[0]
    qs = jnp.concatenate([q_ref[:, g * LANES:(g + 1) * LANES] for g in range(GQA_GROUP)], axis=0)
    rows = GQA_GROUP * tq
    n_chunks = kt_ref.shape[1] // tk
    acc0 = jnp.zeros((rows, LANES), F32)

    def scores(j):
        off = pl.multiple_of(j * tk, tk)
        return _dot(qs, kt_ref[:, pl.ds(off, tk)]), v_ref[pl.ds(off, tk), :]

    if online:
        def body(j, carry):
            m, acc = carry
            s, v = scores(j)
            m_new = jnp.maximum(m, jnp.max(s, axis=1, keepdims=True))
            p = jnp.exp2(s - m_new).astype(BF16)
            return m_new, acc * jnp.exp2(m - m_new) + _dot(p, v)

        m0 = jnp.full((rows, 1), -jnp.inf, F32)
        _, acc = lax.fori_loop(0, n_chunks, body, (m0, acc0))
    else:
        def body(j, acc):
            s, v = scores(j)
            return acc + _dot(jnp.exp2(s).astype(BF16), v)

        acc = lax.fori_loop(0, n_chunks, body, acc0)
    out = acc[:, :HEAD_DIM] / acc[:, HEAD_DIM:HEAD_DIM + 1]
    o_ref[...] = jnp.concatenate([out[g * tq:(g + 1) * tq] for g in range(GQA_GROUP)],
                                 axis=1).astype(BF16)


def _attn(q, kt, v, *, tq, tk, online):
    s = q.shape[0]
    gw = GQA_GROUP * HEAD_DIM
    return pl.pallas_call(
        functools.partial(_attn_kernel, tk=tk, online=online),
        grid=(N_KV_HEADS, s // tq),
        in_specs=[pl.BlockSpec((tq, GQA_GROUP * LANES), lambda h, i: (i, h)),
                  pl.BlockSpec((LANES, s), lambda h, i: (h, 0)),
                  pl.BlockSpec((s, LANES), lambda h, i: (0, h))],
        out_specs=pl.BlockSpec((tq, gw), lambda h, i: (i, h)),
        out_shape=jax.ShapeDtypeStruct((s, N_KV_HEADS * gw), BF16),
        compiler_params=_cparams(2),
        name="attn_online" if online else "attn",
    )(q, kt, v)


def _memkv_kernel(mem_ref, g_ref, wk_ref, wv_ref, ktm_ref, vm_ref):
    mn = _rms(mem_ref[...], g_ref[...]).astype(BF16)
    kt = _dot(mn, wk_ref[...]).T
    v = _dot(mn, wv_ref[...])
    feat_row = lax.broadcasted_iota(jnp.int32, kt.shape, 0) // HEAD_DIM
    feat_col = lax.broadcasted_iota(jnp.int32, v.shape, 1) // HEAD_DIM
    for h in range(N_CROSS_HEADS):
        ktm_ref[h] = jnp.where(feat_row == h, kt, 0.0).astype(BF16)
        vm_ref[h] = jnp.where(feat_col == h, v, 0.0).astype(BF16)


def _memkv(mem, g, wk, wv):
    m, d = mem.shape
    cw = wk.shape[1]
    whole = lambda shape: pl.BlockSpec(shape, lambda: (0,) * len(shape))
    return pl.pallas_call(
        _memkv_kernel,
        in_specs=[whole((m, d)), whole((1, d)), whole((d, cw)), whole((d, cw))],
        out_specs=[whole((N_CROSS_HEADS, cw, m)), whole((N_CROSS_HEADS, m, cw))],
        out_shape=[jax.ShapeDtypeStruct((N_CROSS_HEADS, cw, m), BF16),
                   jax.ShapeDtypeStruct((N_CROSS_HEADS, m, cw), BF16)],
        compiler_params=pltpu.CompilerParams(vmem_limit_bytes=VMEM_LIMIT),
        name="memkv",
    )(mem, g, wk, wv)


def _merge_kernel(h_ref, fm_ref, o_ref, sf_ref, sa_ref, wfb_ref, wab_ref, wout_ref, gc_ref,
                  wcq_ref, ktm_ref, vm_ref, wco_ref, out_ref):
    y_f = _dot(fm_ref[...], wfb_ref[...])
    y_a = _dot(o_ref[...], wab_ref[...])
    merged = (sf_ref[...] * y_f + sa_ref[...] * y_a).astype(BF16)
    h = h_ref[...] + _dot(merged, wout_ref[...])
    n = _rms(h, gc_ref[...]).astype(BF16)
    qc = (_dot(n, wcq_ref[...]) * (1.0 / math.sqrt(HEAD_DIM))).astype(BF16)
    oc = None
    for hh in range(N_CROSS_HEADS):
        s = _dot(qc, ktm_ref[hh])
        e = jnp.exp(s - jnp.max(s, axis=1, keepdims=True))
        p = (e / jnp.sum(e, axis=1, keepdims=True)).astype(BF16)
        head = _dot(p, vm_ref[hh])
        oc = head if oc is None else oc + head
    out_ref[...] = h + _dot(oc.astype(BF16), wco_ref[...])


def _merge(h, fm, o, sf, sa, wfb, wab, wout, gc, wcq, ktm, vm, wco, *, tm):
    s, d = h.shape

    def rows(w):
        return pl.BlockSpec((tm, w), lambda i: (i, 0))

    weights = [wfb, wab, wout, gc, wcq, ktm, vm, wco]
    return pl.pallas_call(
        _merge_kernel,
        grid=(s // tm,),
        in_specs=[rows(d), rows(fm.shape[1]), rows(o.shape[1]), rows(d), rows(d)]
                 + [_resident(w.shape) for w in weights],
        out_specs=rows(d),
        out_shape=jax.ShapeDtypeStruct((s, d), F32),
        compiler_params=_cparams(1),
        name="merge",
    )(h, fm, o, sf, sa, *weights)


def _rope_tables(s):
    t = jnp.arange(s, dtype=jnp.int32)
    row = (t // GRID_W).astype(F32)
    col = (t % GRID_W).astype(F32)
    inv = 1.0 / (ROPE_THETA ** (jnp.arange(0, ROPE_AXIS_DIM, 2, dtype=F32) / ROPE_AXIS_DIM))
    ang_r = row[:, None] * inv[None, :]
    ang_c = col[:, None] * inv[None, :]
    ang = jnp.concatenate([ang_r, ang_r, ang_c, ang_c] * (LANES // HEAD_DIM), axis=1)
    sign = np.where((np.arange(LANES) % ROPE_AXIS_DIM) < ROPE_HALF, -1.0, 1.0).astype(np.float32)
    return jnp.cos(ang), jnp.sin(ang) * sign[None, :]


def _pick_tile(s, want):
    t = min(want, s)
    while s % t:
        t //= 2
    return t


def kernel(x, mem, g_ffn1, w1_gate, w1_up, w1_down, g_mix, w_in, q_gain, k_gain, w_fourier_branch, w_attn_branch, w_out, g_cross, g_mem, w_cq, w_ck, w_cv, w_co, g_ffn2, w2_gate, w2_up, w2_down, g_final):
    batch, s, d = x.shape
    depth = w_in.shape[0]
    qw = w_attn_branch.shape[1]
    kw = N_KV_HEADS * HEAD_DIM
    fw = w_fourier_branch.shape[1]
    vw = N_KV_HEADS * LANES
    assert s % (FFT_INNER * 8) == 0 and s % GRID_W == 0
    assert qw == N_KV_HEADS * GQA_GROUP * HEAD_DIM and fw == FOURIER_GROUPS * FOURIER_GROUP_DIM
    n1 = s // FFT_INNER
    tm = _pick_tile(s, 512)
    row = lambda g: g.reshape(1, -1).astype(F32)
    bf = lambda w: w.astype(BF16)

    cos, sin = _rope_tables(s)
    cs1, cs2, mix = _dft_tables(s)
    tw_c, tw_s = _twiddles(s)
    lane = np.arange(LANES)
    head_mean = jnp.asarray((lane[:, None] // HEAD_DIM == lane[None, :] // HEAD_DIM) / HEAD_DIM,
                            dtype=F32).astype(BF16)
    vone = jnp.asarray((np.arange(vw) % LANES == HEAD_DIM).astype(np.float32)).reshape(1, vw)

    outs = []
    for b in range(batch):
        h = x[b]
        for l in range(depth):
            h = _ffn(h, row(g_ffn1[l]), bf(w1_gate[l]), bf(w1_up[l]), bf(w1_down[l]),
                     row(g_final), final_norm=False, tm=tm)

            w = w_in[l]
            wv = w[:, qw + kw:qw + 2 * kw].reshape(d, N_KV_HEADS, HEAD_DIM)
            wv = jnp.pad(wv, ((0, 0), (0, 0), (0, LANES - HEAD_DIM))).reshape(d, vw)
            win = bf(jnp.concatenate([w[:, :qw + kw], wv, w[:, qw + 2 * kw:]], axis=1))
            shift = (LOG2E * math.sqrt(HEAD_DIM)) * jnp.max(jnp.abs(q_gain[l])) * jnp.max(jnp.abs(k_gain[l]))
            extra_lane = jnp.asarray((lane == HEAD_DIM).astype(np.float32)).reshape(1, LANES)
            q, kt, v, f, sf, sa = _inproj(
                h, row(g_mix[l]), win, row(jnp.tile(q_gain[l], LANES // HEAD_DIM)),
                row(jnp.tile(k_gain[l], LANES // HEAD_DIM)), cos, sin, head_mean,
                extra_lane, -shift * extra_lane, vone, widths=(qw, kw, vw, fw, d), tm=tm)

            a = _fft1(f.reshape(n1, FFT_INNER * fw), cs1, tn=_pick_tile(FFT_INNER * fw, 8192))
            fm = _fft2(a.reshape(2 * n1, FFT_INNER, fw), tw_c, tw_s, cs2, mix, kb=8)
            fm = fm.reshape(s, fw)

            tq = _pick_tile(s, 128)
            o = lax.cond(
                shift <= MAX_SAFE_SHIFT,
                functools.partial(_attn, tq=tq, tk=_pick_tile(s, 2048), online=False),
                functools.partial(_attn, tq=tq, tk=_pick_tile(s, 512), online=True),
                q, kt, v)

            ktm, vm = _memkv(mem[b], row(g_mem[l]), bf(w_ck[l]), bf(w_cv[l]))
            h = _merge(h, fm, o, sf, sa, bf(w_fourier_branch[l]), bf(w_attn_branch[l]), bf(w_out[l]),
                       row(g_cross[l]), bf(w_cq[l]), ktm, vm, bf(w_co[l]), tm=tm)

            h = _ffn(h, row(g_ffn2[l]), bf(w2_gate[l]), bf(w2_up[l]), bf(w2_down[l]),
                     row(g_final), final_norm=(l == depth - 1), tm=tm)
        outs.append(h)
    return jnp.stack(outs, axis=0)
```

```python
import functools
import math

import numpy as np
import jax
import jax.numpy as jnp
from jax import lax
from jax.experimental import pallas as pl
from jax.experimental.pallas import tpu as pltpu

F32 = jnp.float32
BF16 = jnp.bfloat16

LANES = 128
MXU_WIDTH = 256
GRID_W = 64
HEAD_DIM = 64
N_KV_HEADS = 4
GQA_GROUP = 4
N_CROSS_HEADS = 4
FOURIER_GROUPS = 4
FOURIER_GROUP_DIM = 128
ROPE_AXIS_DIM = HEAD_DIM // 2
ROPE_HALF = ROPE_AXIS_DIM // 2
ROPE_THETA = 10000.0
RMS_EPS = 1e-6
FFT_INNER = 128
VMEM_LIMIT = 56 * 1024 * 1024
LOG2E = math.log2(math.e)
MAX_SAFE_SHIFT = 40.0


def _cparams(n_axes):
    return pltpu.CompilerParams(dimension_semantics=("arbitrary",) * n_axes,
                                vmem_limit_bytes=VMEM_LIMIT)


def _resident(shape):
    zeros = (0,) * len(shape)
    return pl.BlockSpec(shape, lambda *_: zeros, pipeline_mode=pl.Buffered(1))


def _dot(a, b):
    return jnp.dot(a, b, preferred_element_type=F32)


def _rms(x, g):
    y = x * lax.rsqrt(jnp.mean(x * x, axis=-1, keepdims=True) + RMS_EPS)
    return y * g


def _ffn_kernel(x_ref, g_ref, wg_ref, wu_ref, wd_ref, gfin_ref, o_ref, *, final_norm):
    x = x_ref[...]
    xn = _rms(x, g_ref[...]).astype(BF16)
    gate = _dot(xn, wg_ref[...])
    up = _dot(xn, wu_ref[...])
    act = (gate * jax.nn.sigmoid(gate) * up).astype(BF16)
    h = x + 0.5 * _dot(act, wd_ref[...])
    if final_norm:
        h = _rms(h, gfin_ref[...])
    o_ref[...] = h


def _ffn(x, g, wg, wu, wd, gfin, *, final_norm, tm):
    s, d = x.shape
    dff = wg.shape[1]
    row = pl.BlockSpec((tm, d), lambda i: (i, 0))
    return pl.pallas_call(
        functools.partial(_ffn_kernel, final_norm=final_norm),
        grid=(s // tm,),
        in_specs=[row, _resident((1, d)), _resident((d, dff)), _resident((d, dff)),
                  _resident((dff, d)), _resident((1, d))],
        out_specs=row,
        out_shape=jax.ShapeDtypeStruct((s, d), F32),
        compiler_params=_cparams(1),
        name="ffn_final" if final_norm else "ffn",
    )(x, g, wg, wu, wd, gfin)


def _head_norm_rope(x, gain, head_mean, cos, sin, first_half):
    ms = _dot((x * x).astype(BF16), head_mean)
    halves = []
    for c in range(x.shape[1] // LANES):
        sl = slice(c * LANES, (c + 1) * LANES)
        y = x[:, sl] * lax.rsqrt(ms[:, sl] + RMS_EPS) * gain
        partner = jnp.where(first_half,
                            pltpu.roll(y, LANES - ROPE_HALF, 1),
                            pltpu.roll(y, ROPE_HALF, 1))
        halves.append(y * cos + partner * sin)
    return halves


def _head_slots(xc, extra, low):
    return jnp.where(low, xc, extra), jnp.where(low, pltpu.roll(xc, HEAD_DIM, 1), extra)


def _inproj_kernel(h_ref, g_ref, win_ref, wv_ref, qg_ref, kg_ref, cos_ref, sin_ref, hm_ref, qx_ref,
                   kx_ref, vone_ref, q_ref, kt_ref, v_ref, f_ref, sf_ref, sa_ref, *, widths):
    qw, kw, fw, gw = widths
    n = _rms(h_ref[...], g_ref[...]).astype(BF16)
    cos = cos_ref[...]
    sin = sin_ref[...]
    head_mean = hm_ref[...]
    lane = lax.broadcasted_iota(jnp.int32, cos.shape, 1)
    first_half = (lane % ROPE_AXIS_DIM) < ROPE_HALF
    low = lane < HEAD_DIM

    def proj(lo, width):
        return _dot(n, win_ref[:, lo:lo + width])

    q = proj(0, qw)
    pair = head_mean.shape[0]
    slot_lo = 0
    for c in range(qw // pair):
        for qc in _head_norm_rope(q[:, c * pair:(c + 1) * pair], qg_ref[...], head_mean, cos, sin,
                                  first_half):
            for slot in _head_slots(qc * (LOG2E / math.sqrt(HEAD_DIM)), qx_ref[...], low):
                q_ref[:, slot_lo:slot_lo + LANES] = slot.astype(BF16)
                slot_lo += LANES
    k = proj(qw, kw)
    k_slots = []
    for c in range(kw // pair):
        for kc in _head_norm_rope(k[:, c * pair:(c + 1) * pair], kg_ref[...], head_mean, cos, sin,
                                  first_half):
            k_slots.extend(_head_slots(kc, kx_ref[...], low))
    kt_ref[...] = jnp.concatenate(k_slots, axis=1).T.astype(BF16)
    v_ref[...] = (_dot(n, wv_ref[...]) + vone_ref[...]).astype(BF16)
    lo = qw + 2 * kw
    f_ref[...] = proj(lo, fw).astype(BF16)
    lo += fw
    sf_ref[...] = jax.nn.sigmoid(proj(lo, gw))
    lo += gw
    sa_ref[...] = jax.nn.sigmoid(proj(lo, gw))


def _inproj(h, g, win, wv, qg, kg, cos, sin, head_mean, qx, kx, vone, *, widths, tm):
    s, d = h.shape
    qw, kw, fw, gw = widths
    vw = wv.shape[1]
    slots = LANES // HEAD_DIM

    def rows(w):
        return pl.BlockSpec((tm, w), lambda i: (i, 0))

    lane_row = _resident((1, LANES))
    return pl.pallas_call(
        functools.partial(_inproj_kernel, widths=widths),
        grid=(s // tm,),
        in_specs=[rows(d), _resident((1, d)), _resident(win.shape), _resident(wv.shape), lane_row,
                  lane_row, rows(LANES), rows(LANES), _resident(head_mean.shape), lane_row, lane_row,
                  _resident((1, vw))],
        out_specs=[rows(qw * slots), pl.BlockSpec((kw * slots, tm), lambda i: (0, i)), rows(vw),
                   rows(fw), rows(gw), rows(gw)],
        out_shape=[jax.ShapeDtypeStruct((s, qw * slots), BF16),
                   jax.ShapeDtypeStruct((kw * slots, s), BF16),
                   jax.ShapeDtypeStruct((s, vw), BF16), jax.ShapeDtypeStruct((s, fw), BF16),
                   jax.ShapeDtypeStruct((s, gw), F32), jax.ShapeDtypeStruct((s, gw), F32)],
        compiler_params=_cparams(1),
        name="inproj",
    )(h, g, win, wv, qg, kg, cos, sin, head_mean, qx, kx, vone)


def _fft1_kernel(x_ref, cs1_ref, a_ref):
    a_ref[...] = _dot(cs1_ref[...], x_ref[...]).astype(a_ref.dtype)


def _fft1(x2d, cs1, *, tn):
    n1, cols = x2d.shape
    return pl.pallas_call(
        _fft1_kernel,
        grid=(cols // tn,),
        in_specs=[pl.BlockSpec((n1, tn), lambda j: (0, j)), _resident((2 * n1, n1))],
        out_specs=pl.BlockSpec((2 * n1, tn), lambda j: (0, j)),
        out_shape=jax.ShapeDtypeStruct((2 * n1, cols), BF16),
        compiler_params=_cparams(1),
        name="fft1",
    )(x2d, cs1)


def _fft2_kernel(ac_ref, as_ref, tc_ref, ts_ref, rot2_ref, mix_ref, o_ref):
    kb, n2, width = ac_ref.shape
    reps = width // LANES
    for kk in range(kb):
        ac = ac_ref[kk].astype(F32)
        as_ = as_ref[kk].astype(F32)
        tc = jnp.concatenate([tc_ref[kk]] * reps, axis=1)
        ts = jnp.concatenate([ts_ref[kk]] * reps, axis=1)
        b = jnp.concatenate([ac * tc - as_ * ts, as_ * tc + ac * ts], axis=0).astype(BF16)
        p = _dot(rot2_ref[...], b).astype(BF16)
        gd = FOURIER_GROUP_DIM
        stacked = jnp.concatenate(
            [jnp.concatenate([p[:n2, g * gd:(g + 1) * gd], p[n2:, g * gd:(g + 1) * gd]], axis=1)
             for g in range(width // gd)], axis=0)
        mixed = _dot(stacked, mix_ref[...]).astype(BF16)
        for g in range(width // gd):
            lo = kk * width + g * gd
            o_ref[:, lo:lo + gd] = mixed[g * n2:(g + 1) * n2]


def _fft2(a3, tc, ts, cs2, mix, *, kb):
    two_n1, n2, width = a3.shape
    n1 = two_n1 // 2
    nblk = n1 // kb
    return pl.pallas_call(
        _fft2_kernel,
        grid=(nblk,),
        in_specs=[pl.BlockSpec((kb, n2, width), lambda j: (j, 0, 0)),
                  pl.BlockSpec((kb, n2, width), lambda j: (j + nblk, 0, 0)),
                  pl.BlockSpec((kb, n2, LANES), lambda j: (j, 0, 0)),
                  pl.BlockSpec((kb, n2, LANES), lambda j: (j, 0, 0)),
                  _resident(cs2.shape), _resident(mix.shape)],
        out_specs=pl.BlockSpec((n2, kb * width), lambda j: (0, j)),
        out_shape=jax.ShapeDtypeStruct((n2, n1 * width), BF16),
        compiler_params=_cparams(1),
        name="fft2",
    )(a3, a3, tc, ts, cs2, mix)


def _dft_tables(s):
    n1 = s // FFT_INNER

    def cos_sin(n, scale):
        ang = 2.0 * np.pi * (np.outer(np.arange(n), np.arange(n)) % n) / n
        return np.cos(ang) * scale, np.sin(ang) * scale

    c1, s1 = cos_sin(n1, n1 ** -0.5)
    c2, s2 = cos_sin(FFT_INNER, FFT_INNER ** -0.5)
    cc, sc = cos_sin(FOURIER_GROUP_DIM, FOURIER_GROUP_DIM ** -0.5)
    as_bf16 = lambda a: jnp.asarray(a, dtype=F32).astype(BF16)
    return (as_bf16(np.concatenate([c1, s1], axis=0)), as_bf16(np.block([[c2, -s2], [s2, c2]])),
            as_bf16(np.concatenate([cc, -sc], axis=0)))


def _twiddles(s):
    n1 = s // FFT_INNER
    prod = (jnp.arange(n1, dtype=jnp.int32)[:, None] * jnp.arange(FFT_INNER, dtype=jnp.int32)[None, :]) % s
    ang = prod.astype(F32) * (2.0 * math.pi / s)
    shape = (n1, FFT_INNER, LANES)
    return (jnp.broadcast_to(jnp.cos(ang)[:, :, None], shape),
            jnp.broadcast_to(jnp.sin(ang)[:, :, None], shape))


def _attn_kernel(q_ref, kt_ref, v_ref, o_ref, *, tk, online):
    tq = q_ref.shape[0]
    qs = jnp.concatenate([q_ref[:, g * LANES:(g + 1) * LANES] for g in range(GQA_GROUP)], axis=0)
    rows = GQA_GROUP * tq
    n_chunks = kt_ref.shape[1] // tk
    acc0 = jnp.zeros((rows, LANES), F32)

    def scores(j):
        off = pl.multiple_of(j * tk, tk)
        return _dot(qs, kt_ref[:, pl.ds(off, tk)]), v_ref[pl.ds(off, tk), :]

    if online:
        def body(j, carry):
            m, acc = carry
            s, v = scores(j)
            m_new = jnp.maximum(m, jnp.max(s, axis=1, keepdims=True))
            p = jnp.exp2(s - m_new).astype(BF16)
            return m_new, acc * jnp.exp2(m - m_new) + _dot(p, v)

        m0 = jnp.full((rows, 1), -jnp.inf, F32)
        _, acc = lax.fori_loop(0, n_chunks, body, (m0, acc0))
    else:
        def body(j, acc):
            s, v = scores(j)
            return acc + _dot(jnp.exp2(s).astype(BF16), v)

        acc = lax.fori_loop(0, n_chunks, body, acc0)
    out = acc[:, :HEAD_DIM] / acc[:, HEAD_DIM:HEAD_DIM + 1]
    o_ref[...] = jnp.concatenate([out[g * tq:(g + 1) * tq] for g in range(GQA_GROUP)],
                                 axis=1).astype(BF16)


def _attn(q, kt, v, *, tq, tk, online):
    s = q.shape[0]
    gw = GQA_GROUP * HEAD_DIM
    return pl.pallas_call(
        functools.partial(_attn_kernel, tk=tk, online=online),
        grid=(N_KV_HEADS, s // tq),
        in_specs=[pl.BlockSpec((tq, GQA_GROUP * LANES), lambda h, i: (i, h)),
                  pl.BlockSpec((LANES, s), lambda h, i: (h, 0)),
                  pl.BlockSpec((s, LANES), lambda h, i: (0, h))],
        out_specs=pl.BlockSpec((tq, gw), lambda h, i: (i, h)),
        out_shape=jax.ShapeDtypeStruct((s, N_KV_HEADS * gw), BF16),
        compiler_params=_cparams(2),
        name="attn_online" if online else "attn",
    )(q, kt, v)


def _memkv_kernel(mem_ref, g_ref, wk_ref, wv_ref, ktm_ref, vm_ref):
    mn = _rms(mem_ref[...], g_ref[...]).astype(BF16)
    kt = _dot(mn, wk_ref[...]).T
    v = _dot(mn, wv_ref[...])
    feat_row = lax.broadcasted_iota(jnp.int32, kt.shape, 0) // HEAD_DIM
    feat_col = lax.broadcasted_iota(jnp.int32, v.shape, 1) // HEAD_DIM
    for h in range(N_CROSS_HEADS):
        ktm_ref[h] = jnp.where(feat_row == h, kt, 0.0).astype(BF16)
        vm_ref[h] = jnp.where(feat_col == h, v, 0.0).astype(BF16)


def _memkv(mem, g, wk, wv):
    m, d = mem.shape
    cw = wk.shape[1]
    whole = lambda shape: pl.BlockSpec(shape, lambda: (0,) * len(shape))
    return pl.pallas_call(
        _memkv_kernel,
        in_specs=[whole((m, d)), whole((1, d)), whole((d, cw)), whole((d, cw))],
        out_specs=[whole((N_CROSS_HEADS, cw, m)), whole((N_CROSS_HEADS, m, cw))],
        out_shape=[jax.ShapeDtypeStruct((N_CROSS_HEADS, cw, m), BF16),
                   jax.ShapeDtypeStruct((N_CROSS_HEADS, m, cw), BF16)],
        compiler_params=pltpu.CompilerParams(vmem_limit_bytes=VMEM_LIMIT),
        name="memkv",
    )(mem, g, wk, wv)


def _merge_kernel(h_ref, fm_ref, o_ref, sf_ref, sa_ref, wfb_ref, wab_ref, wout_ref, gc_ref,
                  wcq_ref, ktm_ref, vm_ref, wco_ref, out_ref):
    y_f = _dot(fm_ref[...], wfb_ref[...])
    y_a = _dot(o_ref[...], wab_ref[...])
    merged = (sf_ref[...] * y_f + sa_ref[...] * y_a).astype(BF16)
    h = h_ref[...] + _dot(merged, wout_ref[...])
    n = _rms(h, gc_ref[...]).astype(BF16)
    qc = (_dot(n, wcq_ref[...]) * (1.0 / math.sqrt(HEAD_DIM))).astype(BF16)
    oc = None
    for hh in range(N_CROSS_HEADS):
        s = _dot(qc, ktm_ref[hh])
        e = jnp.exp(s - jnp.max(s, axis=1, keepdims=True))
        p = (e / jnp.sum(e, axis=1, keepdims=True)).astype(BF16)
        head = _dot(p, vm_ref[hh])
        oc = head if oc is None else oc + head
    out_ref[...] = h + _dot(oc.astype(BF16), wco_ref[...])


def _merge(h, fm, o, sf, sa, wfb, wab, wout, gc, wcq, ktm, vm, wco, *, tm):
    s, d = h.shape

    def rows(w):
        return pl.BlockSpec((tm, w), lambda i: (i, 0))

    weights = [wfb, wab, wout, gc, wcq, ktm, vm, wco]
    return pl.pallas_call(
        _merge_kernel,
        grid=(s // tm,),
        in_specs=[rows(d), rows(fm.shape[1]), rows(o.shape[1]), rows(d), rows(d)]
                 + [_resident(w.shape) for w in weights],
        out_specs=rows(d),
        out_shape=jax.ShapeDtypeStruct((s, d), F32),
        compiler_params=_cparams(1),
        name="merge",
    )(h, fm, o, sf, sa, *weights)


def _rope_tables(s):
    n_rows = s // GRID_W
    inv = 1.0 / (ROPE_THETA ** (jnp.arange(0, ROPE_AXIS_DIM, 2, dtype=F32) / ROPE_AXIS_DIM))
    ang_r = jnp.arange(n_rows, dtype=F32)[:, None] * inv[None, :]
    ang_c = jnp.arange(GRID_W, dtype=F32)[:, None] * inv[None, :]
    full = (n_rows, GRID_W, ROPE_HALF)

    def expand(fn, sign):
        r = jnp.broadcast_to(fn(ang_r)[:, None, :], full)
        c = jnp.broadcast_to(fn(ang_c)[None, :, :], full)
        head = jnp.concatenate([sign * r, r, sign * c, c], axis=-1)
        return jnp.concatenate([head] * (LANES // HEAD_DIM), axis=-1).reshape(s, LANES)

    return expand(jnp.cos, 1.0), expand(jnp.sin, -1.0)


def _pick_tile(s, want):
    t = min(want, s)
    while s % t:
        t //= 2
    return t


def kernel(x, mem, g_ffn1, w1_gate, w1_up, w1_down, g_mix, w_in, q_gain, k_gain, w_fourier_branch, w_attn_branch, w_out, g_cross, g_mem, w_cq, w_ck, w_cv, w_co, g_ffn2, w2_gate, w2_up, w2_down, g_final):
    batch, s, d = x.shape
    depth = w_in.shape[0]
    qw = w_attn_branch.shape[1]
    kw = N_KV_HEADS * HEAD_DIM
    fw = w_fourier_branch.shape[1]
    vw = N_KV_HEADS * LANES
    assert s % (FFT_INNER * 8) == 0 and s % GRID_W == 0
    assert qw == N_KV_HEADS * GQA_GROUP * HEAD_DIM and fw == FOURIER_GROUPS * FOURIER_GROUP_DIM
    n1 = s // FFT_INNER
    tm = _pick_tile(s, 512)
    row = lambda g: g.reshape(1, -1).astype(F32)
    bf = lambda w: w.astype(BF16)

    cos, sin = _rope_tables(s)
    cs1, cs2, mix = _dft_tables(s)
    tw_c, tw_s = _twiddles(s)
    lane = np.arange(LANES)
    mxu_lane = np.arange(MXU_WIDTH)
    head_mean = jnp.asarray((mxu_lane[:, None] // HEAD_DIM == mxu_lane[None, :] // HEAD_DIM) / HEAD_DIM,
                            dtype=F32).astype(BF16)
    vone = jnp.asarray((np.arange(vw) % LANES == HEAD_DIM).astype(np.float32)).reshape(1, vw)

    outs = []
    for b in range(batch):
        h = x[b]
        for l in range(depth):
            h = _ffn(h, row(g_ffn1[l]), bf(w1_gate[l]), bf(w1_up[l]), bf(w1_down[l]),
                     row(g_final), final_norm=False, tm=tm)

            wv = bf(w_in[l][:, qw + kw:qw + 2 * kw]).reshape(d, N_KV_HEADS, HEAD_DIM)
            wv = jnp.pad(wv, ((0, 0), (0, 0), (0, LANES - HEAD_DIM))).reshape(d, vw)
            shift = (LOG2E * math.sqrt(HEAD_DIM)) * jnp.max(jnp.abs(q_gain[l])) * jnp.max(jnp.abs(k_gain[l]))
            extra_lane = jnp.asarray((lane == HEAD_DIM).astype(np.float32)).reshape(1, LANES)
            q, kt, v, f, sf, sa = _inproj(
                h, row(g_mix[l]), bf(w_in[l]), wv, row(jnp.tile(q_gain[l], LANES // HEAD_DIM)),
                row(jnp.tile(k_gain[l], LANES // HEAD_DIM)), cos, sin, head_mean,
                extra_lane, -shift * extra_lane, vone, widths=(qw, kw, fw, d), tm=tm)

            a = _fft1(f.reshape(n1, FFT_INNER * fw), cs1, tn=_pick_tile(FFT_INNER * fw, 8192))
            fm = _fft2(a.reshape(2 * n1, FFT_INNER, fw), tw_c, tw_s, cs2, mix, kb=8)
            fm = fm.reshape(s, fw)

            o = lax.cond(
                shift <= MAX_SAFE_SHIFT,
                functools.partial(_attn, tq=_pick_tile(s, 512), tk=_pick_tile(s, 4096), online=False),
                functools.partial(_attn, tq=_pick_tile(s, 128), tk=_pick_tile(s, 512), online=True),
                q, kt, v)

            ktm, vm = _memkv(mem[b], row(g_mem[l]), bf(w_ck[l]), bf(w_cv[l]))
            h = _merge(h, fm, o, sf, sa, bf(w_fourier_branch[l]), bf(w_attn_branch[l]), bf(w_out[l]),
                       row(g_cross[l]), bf(w_cq[l]), ktm, vm, bf(w_co[l]), tm=tm)

            h = _ffn(h, row(g_ffn2[l]), bf(w2_gate[l]), bf(w2_up[l]), bf(w2_down[l]),
                     row(g_final), final_norm=(l == depth - 1), tm=tm)
        outs.append(h)
    return jnp.stack(outs, axis=0)
```

```python
import functools
import math

import numpy as np
import jax
import jax.numpy as jnp
from jax import lax
from jax.experimental import pallas as pl
from jax.experimental.pallas import tpu as pltpu

F32 = jnp.float32
BF16 = jnp.bfloat16

LANES = 128
MXU_WIDTH = 256
GRID_W = 64
HEAD_DIM = 64
N_KV_HEADS = 4
GQA_GROUP = 4
N_CROSS_HEADS = 4
FOURIER_GROUPS = 4
FOURIER_GROUP_DIM = 128
ROPE_AXIS_DIM = HEAD_DIM // 2
ROPE_HALF = ROPE_AXIS_DIM // 2
ROPE_THETA = 10000.0
RMS_EPS = 1e-6
FFT_INNER = 128
VMEM_LIMIT = 56 * 1024 * 1024
LOG2E = math.log2(math.e)
MAX_SAFE_SHIFT = 40.0


def _cparams(n_axes):
    return pltpu.CompilerParams(dimension_semantics=("arbitrary",) * n_axes,
                                vmem_limit_bytes=VMEM_LIMIT)


def _resident(shape):
    zeros = (0,) * len(shape)
    return pl.BlockSpec(shape, lambda *_: zeros, pipeline_mode=pl.Buffered(1))


def _dot(a, b):
    return jnp.dot(a, b, preferred_element_type=F32)


def _rms(x, g):
    y = x * lax.rsqrt(jnp.mean(x * x, axis=-1, keepdims=True) + RMS_EPS)
    return y * g


def _ffn_kernel(x_ref, g_ref, wg_ref, wu_ref, wd_ref, gfin_ref, o_ref, *, final_norm, sub_tiles):
    sub = x_ref.shape[0] // sub_tiles
    tiles = [slice(r * sub, (r + 1) * sub) for r in range(sub_tiles)]
    x = [x_ref[t, :] for t in tiles]
    xn = [_rms(v, g_ref[...]).astype(BF16) for v in x]
    gate = [_dot(v, wg_ref[...]) for v in xn]
    up = [_dot(v, wu_ref[...]) for v in xn]
    act = [(g * jax.nn.sigmoid(g) * u).astype(BF16) for g, u in zip(gate, up)]
    h = [v + 0.5 * _dot(a, wd_ref[...]) for v, a in zip(x, act)]
    if final_norm:
        h = [_rms(v, gfin_ref[...]) for v in h]
    for t, v in zip(tiles, h):
        o_ref[t, :] = v


def _ffn(x, g, wg, wu, wd, gfin, *, final_norm, tm, sub_tiles):
    s, d = x.shape
    dff = wg.shape[1]
    row = pl.BlockSpec((tm, d), lambda i: (i, 0))
    return pl.pallas_call(
        functools.partial(_ffn_kernel, final_norm=final_norm, sub_tiles=sub_tiles),
        grid=(s // tm,),
        in_specs=[row, _resident((1, d)), _resident((d, dff)), _resident((d, dff)),
                  _resident((dff, d)), _resident((1, d))],
        out_specs=row,
        out_shape=jax.ShapeDtypeStruct((s, d), F32),
        compiler_params=_cparams(1),
        name="ffn_final" if final_norm else "ffn",
    )(x, g, wg, wu, wd, gfin)


def _head_norm_rope(x, gain, head_mean, cos, sin, first_half):
    ms = _dot((x * x).astype(BF16), head_mean)
    halves = []
    for c in range(x.shape[1] // LANES):
        sl = slice(c * LANES, (c + 1) * LANES)
        y = x[:, sl] * lax.rsqrt(ms[:, sl] + RMS_EPS) * gain
        partner = jnp.where(first_half,
                            pltpu.roll(y, LANES - ROPE_HALF, 1),
                            pltpu.roll(y, ROPE_HALF, 1))
        halves.append(y * cos + partner * sin)
    return halves


def _head_slots(xc, extra, low):
    return jnp.where(low, xc, extra), jnp.where(low, pltpu.roll(xc, HEAD_DIM, 1), extra)


def _inproj_kernel(h_ref, g_ref, win_ref, wv_ref, qg_ref, kg_ref, cos_ref, sin_ref, hm_ref, qx_ref,
                   kx_ref, vone_ref, q_ref, kt_ref, v_ref, f_ref, sf_ref, sa_ref, *, widths):
    qw, kw, fw, gw = widths
    n = _rms(h_ref[...], g_ref[...]).astype(BF16)
    cos = cos_ref[...]
    sin = sin_ref[...]
    head_mean = hm_ref[...]
    lane = lax.broadcasted_iota(jnp.int32, cos.shape, 1)
    first_half = (lane % ROPE_AXIS_DIM) < ROPE_HALF
    low = lane < HEAD_DIM

    def proj(lo, width):
        return _dot(n, win_ref[:, lo:lo + width])

    q = proj(0, qw)
    pair = head_mean.shape[0]
    slot_lo = 0
    for c in range(qw // pair):
        for qc in _head_norm_rope(q[:, c * pair:(c + 1) * pair], qg_ref[...], head_mean, cos, sin,
                                  first_half):
            for slot in _head_slots(qc * (LOG2E / math.sqrt(HEAD_DIM)), qx_ref[...], low):
                q_ref[:, slot_lo:slot_lo + LANES] = slot.astype(BF16)
                slot_lo += LANES
    k = proj(qw, kw)
    k_slots = []
    for c in range(kw // pair):
        for kc in _head_norm_rope(k[:, c * pair:(c + 1) * pair], kg_ref[...], head_mean, cos, sin,
                                  first_half):
            k_slots.extend(_head_slots(kc, kx_ref[...], low))
    kt_ref[...] = jnp.concatenate(k_slots, axis=1).T.astype(BF16)
    v_ref[...] = (_dot(n, wv_ref[...]) + vone_ref[...]).astype(BF16)
    lo = qw + 2 * kw
    f_ref[...] = proj(lo, fw).astype(BF16)
    lo += fw
    sf_ref[...] = jax.nn.sigmoid(proj(lo, gw)).astype(sf_ref.dtype)
    lo += gw
    sa_ref[...] = jax.nn.sigmoid(proj(lo, gw)).astype(sa_ref.dtype)


def _inproj(h, g, win, wv, qg, kg, cos, sin, head_mean, qx, kx, vone, *, widths, tm):
    s, d = h.shape
    qw, kw, fw, gw = widths
    vw = wv.shape[1]
    slots = LANES // HEAD_DIM

    def rows(w):
        return pl.BlockSpec((tm, w), lambda i: (i, 0))

    lane_row = _resident((1, LANES))
    return pl.pallas_call(
        functools.partial(_inproj_kernel, widths=widths),
        grid=(s // tm,),
        in_specs=[rows(d), _resident((1, d)), _resident(win.shape), _resident(wv.shape), lane_row,
                  lane_row, rows(LANES), rows(LANES), _resident(head_mean.shape), lane_row, lane_row,
                  _resident((1, vw))],
        out_specs=[rows(qw * slots), pl.BlockSpec((kw * slots, tm), lambda i: (0, i)), rows(vw),
                   rows(fw), rows(gw), rows(gw)],
        out_shape=[jax.ShapeDtypeStruct((s, qw * slots), BF16),
                   jax.ShapeDtypeStruct((kw * slots, s), BF16),
                   jax.ShapeDtypeStruct((s, vw), BF16), jax.ShapeDtypeStruct((s, fw), BF16),
                   jax.ShapeDtypeStruct((s, gw), BF16), jax.ShapeDtypeStruct((s, gw), BF16)],
        compiler_params=_cparams(1),
        name="inproj",
    )(h, g, win, wv, qg, kg, cos, sin, head_mean, qx, kx, vone)


def _fft1_kernel(x_ref, cs1_ref, a_ref):
    a_ref[...] = _dot(cs1_ref[...], x_ref[...]).astype(a_ref.dtype)


def _fft1(x2d, cs1, *, tn):
    n1, cols = x2d.shape
    return pl.pallas_call(
        _fft1_kernel,
        grid=(cols // tn,),
        in_specs=[pl.BlockSpec((n1, tn), lambda j: (0, j)), _resident((2 * n1, n1))],
        out_specs=pl.BlockSpec((2 * n1, tn), lambda j: (0, j)),
        out_shape=jax.ShapeDtypeStruct((2 * n1, cols), BF16),
        compiler_params=_cparams(1),
        name="fft1",
    )(x2d, cs1)


def _fft2_kernel(ac_ref, as_ref, tc_ref, ts_ref, rot2_ref, mix_ref, o_ref):
    kb, n2, width = ac_ref.shape
    for kk in range(kb):
        ac = ac_ref[kk].astype(F32)
        as_ = as_ref[kk].astype(F32)
        tc = jnp.broadcast_to(tc_ref[0][:, kk:kk + 1], (n2, width))
        ts = jnp.broadcast_to(ts_ref[0][:, kk:kk + 1], (n2, width))
        b = jnp.concatenate([ac * tc - as_ * ts, as_ * tc + ac * ts], axis=0).astype(BF16)
        p = _dot(rot2_ref[...], b).astype(BF16)
        gd = FOURIER_GROUP_DIM
        stacked = jnp.concatenate(
            [jnp.concatenate([p[:n2, g * gd:(g + 1) * gd], p[n2:, g * gd:(g + 1) * gd]], axis=1)
             for g in range(width // gd)], axis=0)
        mixed = _dot(stacked, mix_ref[...]).astype(BF16)
        for g in range(width // gd):
            lo = kk * width + g * gd
            o_ref[:, lo:lo + gd] = mixed[g * n2:(g + 1) * n2]


def _fft2(a3, tc, ts, cs2, mix, *, kb):
    two_n1, n2, width = a3.shape
    n1 = two_n1 // 2
    nblk = n1 // kb
    return pl.pallas_call(
        _fft2_kernel,
        grid=(nblk,),
        in_specs=[pl.BlockSpec((kb, n2, width), lambda j: (j, 0, 0)),
                  pl.BlockSpec((kb, n2, width), lambda j: (j + nblk, 0, 0)),
                  pl.BlockSpec((1, n2, kb), lambda j: (j, 0, 0)),
                  pl.BlockSpec((1, n2, kb), lambda j: (j, 0, 0)),
                  _resident(cs2.shape), _resident(mix.shape)],
        out_specs=pl.BlockSpec((n2, kb * width), lambda j: (0, j)),
        out_shape=jax.ShapeDtypeStruct((n2, n1 * width), BF16),
        compiler_params=_cparams(1),
        name="fft2",
    )(a3, a3, tc, ts, cs2, mix)


def _dft_tables(s):
    n1 = s // FFT_INNER

    def cos_sin(n, scale):
        ang = 2.0 * np.pi * (np.outer(np.arange(n), np.arange(n)) % n) / n
        return np.cos(ang) * scale, np.sin(ang) * scale

    c1, s1 = cos_sin(n1, n1 ** -0.5)
    c2, s2 = cos_sin(FFT_INNER, FFT_INNER ** -0.5)
    cc, sc = cos_sin(FOURIER_GROUP_DIM, FOURIER_GROUP_DIM ** -0.5)
    as_bf16 = lambda a: jnp.asarray(a, dtype=F32).astype(BF16)
    return (as_bf16(np.concatenate([c1, s1], axis=0)), as_bf16(np.block([[c2, -s2], [s2, c2]])),
            as_bf16(np.concatenate([cc, -sc], axis=0)))


def _twiddles(s, kb):
    n1 = s // FFT_INNER
    prod = (jnp.arange(n1, dtype=jnp.int32)[:, None] * jnp.arange(FFT_INNER, dtype=jnp.int32)[None, :]) % s
    ang = (prod.astype(F32) * (2.0 * math.pi / s)).reshape(n1 // kb, kb, FFT_INNER).transpose(0, 2, 1)
    return jnp.cos(ang), jnp.sin(ang)


def _attn_kernel(q_ref, kt_ref, v_ref, o_ref, *, tk, online):
    tq = q_ref.shape[0]
    qs = jnp.concatenate([q_ref[:, g * LANES:(g + 1) * LANES] for g in range(GQA_GROUP)], axis=0)
    rows = GQA_GROUP * tq
    n_chunks = kt_ref.shape[1] // tk
    acc0 = jnp.zeros((rows, LANES), F32)

    def scores(j):
        off = pl.multiple_of(j * tk, tk)
        return _dot(qs, kt_ref[:, pl.ds(off, tk)]), v_ref[pl.ds(off, tk), :]

    if online:
        def body(j, carry):
            m, acc = carry
            s, v = scores(j)
            m_new = jnp.maximum(m, jnp.max(s, axis=1, keepdims=True))
            p = jnp.exp2(s - m_new).astype(BF16)
            return m_new, acc * jnp.exp2(m - m_new) + _dot(p, v)

        m0 = jnp.full((rows, 1), -jnp.inf, F32)
        _, acc = lax.fori_loop(0, n_chunks, body, (m0, acc0))
    else:
        def body(j, acc):
            s, v = scores(j)
            return acc + _dot(jnp.exp2(s).astype(BF16), v)

        acc = lax.fori_loop(0, n_chunks, body, acc0)
    out = acc[:, :HEAD_DIM] / acc[:, HEAD_DIM:HEAD_DIM + 1]
    o_ref[...] = jnp.concatenate([out[g * tq:(g + 1) * tq] for g in range(GQA_GROUP)],
                                 axis=1).astype(BF16)


def _attn(q, kt, v, *, tq, tk, online):
    s = q.shape[0]
    gw = GQA_GROUP * HEAD_DIM
    return pl.pallas_call(
        functools.partial(_attn_kernel, tk=tk, online=online),
        grid=(N_KV_HEADS, s // tq),
        in_specs=[pl.BlockSpec((tq, GQA_GROUP * LANES), lambda h, i: (i, h)),
                  pl.BlockSpec((LANES, s), lambda h, i: (h, 0)),
                  pl.BlockSpec((s, LANES), lambda h, i: (0, h))],
        out_specs=pl.BlockSpec((tq, gw), lambda h, i: (i, h)),
        out_shape=jax.ShapeDtypeStruct((s, N_KV_HEADS * gw), BF16),
        compiler_params=_cparams(2),
        name="attn_online" if online else "attn",
    )(q, kt, v)


def _memkv_kernel(mem_ref, g_ref, wk_ref, wv_ref, ktm_ref, vm_ref):
    mn = _rms(mem_ref[...], g_ref[...]).astype(BF16)
    kt = _dot(mn, wk_ref[...]).T
    v = _dot(mn, wv_ref[...])
    feat_row = lax.broadcasted_iota(jnp.int32, kt.shape, 0) // HEAD_DIM
    feat_col = lax.broadcasted_iota(jnp.int32, v.shape, 1) // HEAD_DIM
    for h in range(N_CROSS_HEADS):
        ktm_ref[h] = jnp.where(feat_row == h, kt, 0.0).astype(BF16)
        vm_ref[h] = jnp.where(feat_col == h, v, 0.0).astype(BF16)


def _memkv(mem, g, wk, wv):
    m, d = mem.shape
    cw = wk.shape[1]
    whole = lambda shape: pl.BlockSpec(shape, lambda: (0,) * len(shape))
    return pl.pallas_call(
        _memkv_kernel,
        in_specs=[whole((m, d)), whole((1, d)), whole((d, cw)), whole((d, cw))],
        out_specs=[whole((N_CROSS_HEADS, cw, m)), whole((N_CROSS_HEADS, m, cw))],
        out_shape=[jax.ShapeDtypeStruct((N_CROSS_HEADS, cw, m), BF16),
                   jax.ShapeDtypeStruct((N_CROSS_HEADS, m, cw), BF16)],
        compiler_params=pltpu.CompilerParams(vmem_limit_bytes=VMEM_LIMIT),
        name="memkv",
    )(mem, g, wk, wv)


def _merge_kernel(h_ref, fm_ref, o_ref, sf_ref, sa_ref, wfb_ref, wab_ref, wout_ref, gc_ref,
                  wcq_ref, ktm_ref, vm_ref, wco_ref, out_ref, *, sub_tiles):
    sub = h_ref.shape[0] // sub_tiles
    tiles = [slice(r * sub, (r + 1) * sub) for r in range(sub_tiles)]
    y_f = [_dot(fm_ref[t, :], wfb_ref[...]) for t in tiles]
    y_a = [_dot(o_ref[t, :], wab_ref[...]) for t in tiles]
    merged = [(sf_ref[t, :] * f + sa_ref[t, :] * a).astype(BF16) for t, f, a in zip(tiles, y_f, y_a)]
    h = [h_ref[t, :] + _dot(m, wout_ref[...]) for t, m in zip(tiles, merged)]
    n = [_rms(x, gc_ref[...]).astype(BF16) for x in h]
    qc = [(_dot(x, wcq_ref[...]) * (1.0 / math.sqrt(HEAD_DIM))).astype(BF16) for x in n]
    oc = [None] * sub_tiles
    for hh in range(N_CROSS_HEADS):
        s = [_dot(x, ktm_ref[hh]) for x in qc]
        e = [jnp.exp(x - jnp.max(x, axis=1, keepdims=True)) for x in s]
        p = [(x / jnp.sum(x, axis=1, keepdims=True)).astype(BF16) for x in e]
        head = [_dot(x, vm_ref[hh]) for x in p]
        oc = [x if acc is None else acc + x for acc, x in zip(oc, head)]
    for t, x, o in zip(tiles, h, oc):
        out_ref[t, :] = x + _dot(o.astype(BF16), wco_ref[...])


def _merge(h, fm, o, sf, sa, wfb, wab, wout, gc, wcq, ktm, vm, wco, *, tm, sub_tiles):
    s, d = h.shape

    def rows(w):
        return pl.BlockSpec((tm, w), lambda i: (i, 0))

    weights = [wfb, wab, wout, gc, wcq, ktm, vm, wco]
    return pl.pallas_call(
        functools.partial(_merge_kernel, sub_tiles=sub_tiles),
        grid=(s // tm,),
        in_specs=[rows(d), rows(fm.shape[1]), rows(o.shape[1]), rows(d), rows(d)]
                 + [_resident(w.shape) for w in weights],
        out_specs=rows(d),
        out_shape=jax.ShapeDtypeStruct((s, d), F32),
        compiler_params=_cparams(1),
        name="merge",
    )(h, fm, o, sf, sa, *weights)


def _rope_tables(s):
    n_rows = s // GRID_W
    lane = np.arange(LANES)
    inv = 1.0 / (ROPE_THETA ** (jnp.arange(0, ROPE_AXIS_DIM, 2, dtype=F32) / ROPE_AXIS_DIM))
    inv_lane = inv[lane % ROPE_HALF]
    row_lane = jnp.asarray((lane % HEAD_DIM) < ROPE_AXIS_DIM)
    sign = jnp.asarray(np.where((lane % ROPE_AXIS_DIM) < ROPE_HALF, -1.0, 1.0), dtype=F32)
    ang_r = jnp.arange(n_rows, dtype=F32)[:, None] * inv_lane[None, :]
    ang_c = jnp.arange(GRID_W, dtype=F32)[:, None] * inv_lane[None, :]

    def expand(fn, scale):
        table = jnp.where(row_lane, fn(ang_r)[:, None, :], fn(ang_c)[None, :, :]) * scale
        return table.reshape(s, LANES)

    return expand(jnp.cos, 1.0), expand(jnp.sin, sign)


def _pick_tile(s, want):
    t = min(want, s)
    while s % t:
        t //= 2
    return t


def _tiles(s, fft_cols):
    tm = _pick_tile(s, 512)
    return dict(
        tokens=tm,
        ffn_sub_tiles=1,
        merge_tokens=_pick_tile(s, 2 * tm),
        merge_sub_tiles=4,
        fft_cols=_pick_tile(fft_cols, 8192),
        fft_k1=8,
        attn_q=_pick_tile(s, 512),
        attn_kv=_pick_tile(s, 4096),
        attn_online_q=_pick_tile(s, 128),
        attn_online_kv=_pick_tile(s, 512),
    )


def kernel(x, mem, g_ffn1, w1_gate, w1_up, w1_down, g_mix, w_in, q_gain, k_gain, w_fourier_branch, w_attn_branch, w_out, g_cross, g_mem, w_cq, w_ck, w_cv, w_co, g_ffn2, w2_gate, w2_up, w2_down, g_final):
    batch, s, d = x.shape
    depth = w_in.shape[0]
    qw = w_attn_branch.shape[1]
    kw = N_KV_HEADS * HEAD_DIM
    fw = w_fourier_branch.shape[1]
    vw = N_KV_HEADS * LANES
    assert s % (FFT_INNER * 8) == 0 and s % GRID_W == 0
    assert qw == N_KV_HEADS * GQA_GROUP * HEAD_DIM and fw == FOURIER_GROUPS * FOURIER_GROUP_DIM
    n1 = s // FFT_INNER
    tiles = _tiles(s, FFT_INNER * fw)
    row = lambda g: g.reshape(1, -1).astype(F32)
    bf = lambda w: w.astype(BF16)

    cos, sin = _rope_tables(s)
    cs1, cs2, mix = _dft_tables(s)
    tw_c, tw_s = _twiddles(s, tiles["fft_k1"])
    lane = np.arange(LANES)
    mxu_lane = np.arange(MXU_WIDTH)
    head_mean = jnp.asarray((mxu_lane[:, None] // HEAD_DIM == mxu_lane[None, :] // HEAD_DIM) / HEAD_DIM,
                            dtype=F32).astype(BF16)
    vone = jnp.asarray((np.arange(vw) % LANES == HEAD_DIM).astype(np.float32)).reshape(1, vw)

    outs = []
    for b in range(batch):
        h = x[b]
        for l in range(depth):
            h = _ffn(h, row(g_ffn1[l]), bf(w1_gate[l]), bf(w1_up[l]), bf(w1_down[l]),
                     row(g_final), final_norm=False, tm=tiles["tokens"], sub_tiles=tiles["ffn_sub_tiles"])

            wv = bf(w_in[l][:, qw + kw:qw + 2 * kw]).reshape(d, N_KV_HEADS, HEAD_DIM)
            wv = jnp.pad(wv, ((0, 0), (0, 0), (0, LANES - HEAD_DIM))).reshape(d, vw)
            shift = (LOG2E * math.sqrt(HEAD_DIM)) * jnp.max(jnp.abs(q_gain[l])) * jnp.max(jnp.abs(k_gain[l]))
            extra_lane = jnp.asarray((lane == HEAD_DIM).astype(np.float32)).reshape(1, LANES)
            q, kt, v, f, sf, sa = _inproj(
                h, row(g_mix[l]), bf(w_in[l]), wv, row(jnp.tile(q_gain[l], LANES // HEAD_DIM)),
                row(jnp.tile(k_gain[l], LANES // HEAD_DIM)), cos, sin, head_mean,
                extra_lane, -shift * extra_lane, vone, widths=(qw, kw, fw, d), tm=tiles["tokens"])

            a = _fft1(f.reshape(n1, FFT_INNER * fw), cs1, tn=tiles["fft_cols"])
            fm = _fft2(a.reshape(2 * n1, FFT_INNER, fw), tw_c, tw_s, cs2, mix, kb=tiles["fft_k1"])
            fm = fm.reshape(s, fw)

            o = lax.cond(
                shift <= MAX_SAFE_SHIFT,
                functools.partial(_attn, tq=tiles["attn_q"], tk=tiles["attn_kv"], online=False),
                functools.partial(_attn, tq=tiles["attn_online_q"], tk=tiles["attn_online_kv"],
                                  online=True),
                q, kt, v)

            ktm, vm = _memkv(mem[b], row(g_mem[l]), bf(w_ck[l]), bf(w_cv[l]))
            h = _merge(h, fm, o, sf, sa, bf(w_fourier_branch[l]), bf(w_attn_branch[l]), bf(w_out[l]),
                       row(g_cross[l]), bf(w_cq[l]), ktm, vm, bf(w_co[l]),
                       tm=tiles["merge_tokens"], sub_tiles=tiles["merge_sub_tiles"])

            h = _ffn(h, row(g_ffn2[l]), bf(w2_gate[l]), bf(w2_up[l]), bf(w2_down[l]),
                     row(g_final), final_norm=(l == depth - 1), tm=tiles["tokens"],
                     sub_tiles=tiles["ffn_sub_tiles"])
        outs.append(h)
    return jnp.stack(outs, axis=0)
```

```python
import functools
import math

import numpy as np
import jax
import jax.numpy as jnp
from jax import lax
from jax.experimental import pallas as pl
from jax.experimental.pallas import tpu as pltpu

F32 = jnp.float32
BF16 = jnp.bfloat16

LANES = 128
MXU_WIDTH = 256
GRID_W = 64
HEAD_DIM = 64
N_KV_HEADS = 4
GQA_GROUP = 4
N_CROSS_HEADS = 4
FOURIER_GROUPS = 4
FOURIER_GROUP_DIM = 128
ROPE_AXIS_DIM = HEAD_DIM // 2
ROPE_HALF = ROPE_AXIS_DIM // 2
ROPE_THETA = 10000.0
RMS_EPS = 1e-6
FFT_INNER = 128
VMEM_LIMIT = 56 * 1024 * 1024
LOG2E = math.log2(math.e)
MAX_SAFE_SHIFT = 40.0


def _cparams(n_axes):
    return pltpu.CompilerParams(dimension_semantics=("arbitrary",) * n_axes,
                                vmem_limit_bytes=VMEM_LIMIT)


def _resident(shape):
    zeros = (0,) * len(shape)
    return pl.BlockSpec(shape, lambda *_: zeros, pipeline_mode=pl.Buffered(1))


def _dot(a, b):
    return jnp.dot(a, b, preferred_element_type=F32)


def _rms(x, g):
    y = x * lax.rsqrt(jnp.mean(x * x, axis=-1, keepdims=True) + RMS_EPS)
    return y * g


def _ffn_kernel(x_ref, g_ref, wg_ref, wu_ref, wd_ref, gfin_ref, o_ref, *, final_norm, dff_chunk):
    x = x_ref[...]
    xn = _rms(x, g_ref[...]).astype(BF16)
    dff = wg_ref.shape[1]
    acc = None
    for lo in range(0, dff, dff_chunk):
        hi = min(lo + dff_chunk, dff)
        gate = _dot(xn, wg_ref[:, lo:hi])
        up = _dot(xn, wu_ref[:, lo:hi])
        act = (gate * jax.nn.sigmoid(gate) * up).astype(BF16)
        part = _dot(act, wd_ref[lo:hi, :])
        acc = part if acc is None else acc + part
    h = x + 0.5 * acc
    if final_norm:
        h = _rms(h, gfin_ref[...])
    o_ref[...] = h


def _ffn(x, g, wg, wu, wd, gfin, *, final_norm, tm, dff_chunk):
    s, d = x.shape
    dff = wg.shape[1]
    row = pl.BlockSpec((tm, d), lambda i: (i, 0))
    return pl.pallas_call(
        functools.partial(_ffn_kernel, final_norm=final_norm, dff_chunk=dff_chunk),
        grid=(s // tm,),
        in_specs=[row, _resident((1, d)), _resident((d, dff)), _resident((d, dff)),
                  _resident((dff, d)), _resident((1, d))],
        out_specs=row,
        out_shape=jax.ShapeDtypeStruct((s, d), F32),
        compiler_params=_cparams(1),
        name="ffn_final" if final_norm else "ffn",
    )(x, g, wg, wu, wd, gfin)


def _head_norm_rope(x, gain, head_mean, cos, sin, first_half):
    ms = _dot((x * x).astype(BF16), head_mean)
    halves = []
    for c in range(x.shape[1] // LANES):
        sl = slice(c * LANES, (c + 1) * LANES)
        y = x[:, sl] * lax.rsqrt(ms[:, sl] + RMS_EPS) * gain
        partner = jnp.where(first_half,
                            pltpu.roll(y, LANES - ROPE_HALF, 1),
                            pltpu.roll(y, ROPE_HALF, 1))
        halves.append(y * cos + partner * sin)
    return halves


def _head_slots(xc, extra, low):
    return jnp.where(low, xc, extra), jnp.where(low, pltpu.roll(xc, HEAD_DIM, 1), extra)


def _token_table(row_ref, col_ref):
    col = col_ref[...]
    return jnp.concatenate([jnp.broadcast_to(row_ref[r:r + 1, :], col.shape) + col
                            for r in range(row_ref.shape[0])], axis=0)


def _inproj_kernel(h_ref, g_ref, win_ref, wv_ref, qg_ref, kg_ref, cosr_ref, sinr_ref, cosc_ref,
                   sinc_ref, hm_ref, qx_ref, kx_ref, vone_ref, q_ref, kt_ref, v_ref, f_ref, sf_ref,
                   sa_ref, *, widths):
    qw, kw, fw, gw = widths
    n = _rms(h_ref[...], g_ref[...]).astype(BF16)
    cos = _token_table(cosr_ref, cosc_ref)
    sin = _token_table(sinr_ref, sinc_ref)
    head_mean = hm_ref[...]
    lane = lax.broadcasted_iota(jnp.int32, cos.shape, 1)
    first_half = (lane % ROPE_AXIS_DIM) < ROPE_HALF
    low = lane < HEAD_DIM

    def proj(lo, width):
        return _dot(n, win_ref[:, lo:lo + width])

    q = proj(0, qw)
    pair = head_mean.shape[0]
    slot_lo = 0
    for c in range(qw // pair):
        for qc in _head_norm_rope(q[:, c * pair:(c + 1) * pair], qg_ref[...], head_mean, cos, sin,
                                  first_half):
            for slot in _head_slots(qc * (LOG2E / math.sqrt(HEAD_DIM)), qx_ref[...], low):
                q_ref[:, slot_lo:slot_lo + LANES] = slot.astype(BF16)
                slot_lo += LANES
    k = proj(qw, kw)
    k_slots = []
    for c in range(kw // pair):
        for kc in _head_norm_rope(k[:, c * pair:(c + 1) * pair], kg_ref[...], head_mean, cos, sin,
                                  first_half):
            k_slots.extend(_head_slots(kc, kx_ref[...], low))
    kt_ref[...] = jnp.concatenate(k_slots, axis=1).T.astype(BF16)
    v_ref[...] = (_dot(n, wv_ref[...]) + vone_ref[...]).astype(BF16)
    lo = qw + 2 * kw
    f_ref[...] = proj(lo, fw).astype(BF16)
    lo += fw
    sf_ref[...] = jax.nn.sigmoid(proj(lo, gw)).astype(sf_ref.dtype)
    lo += gw
    sa_ref[...] = jax.nn.sigmoid(proj(lo, gw)).astype(sa_ref.dtype)


def _inproj(h, g, win, wv, qg, kg, rope, head_mean, qx, kx, vone, *, widths, tm):
    s, d = h.shape
    qw, kw, fw, gw = widths
    vw = wv.shape[1]
    slots = LANES // HEAD_DIM
    assert tm % GRID_W == 0

    def rows(w):
        return pl.BlockSpec((tm, w), lambda i: (i, 0))

    lane_row = _resident((1, LANES))
    grid_rows = pl.BlockSpec((tm // GRID_W, LANES), lambda i: (i, 0))
    grid_cols = _resident((GRID_W, LANES))
    return pl.pallas_call(
        functools.partial(_inproj_kernel, widths=widths),
        grid=(s // tm,),
        in_specs=[rows(d), _resident((1, d)), _resident(win.shape), _resident(wv.shape), lane_row,
                  lane_row, grid_rows, grid_rows, grid_cols, grid_cols,
                  _resident(head_mean.shape), lane_row, lane_row, _resident((1, vw))],
        out_specs=[rows(qw * slots), pl.BlockSpec((kw * slots, tm), lambda i: (0, i)), rows(vw),
                   rows(fw), rows(gw), rows(gw)],
        out_shape=[jax.ShapeDtypeStruct((s, qw * slots), BF16),
                   jax.ShapeDtypeStruct((kw * slots, s), BF16),
                   jax.ShapeDtypeStruct((s, vw), BF16), jax.ShapeDtypeStruct((s, fw), BF16),
                   jax.ShapeDtypeStruct((s, gw), BF16), jax.ShapeDtypeStruct((s, gw), BF16)],
        compiler_params=_cparams(1),
        name="inproj",
    )(h, g, win, wv, qg, kg, *rope, head_mean, qx, kx, vone)


def _fft1_kernel(x_ref, cs1_ref, a_ref):
    a_ref[...] = _dot(cs1_ref[...], x_ref[...]).astype(a_ref.dtype)


def _fft1(x2d, cs1, *, tn):
    n1, cols = x2d.shape
    return pl.pallas_call(
        _fft1_kernel,
        grid=(cols // tn,),
        in_specs=[pl.BlockSpec((n1, tn), lambda j: (0, j)), _resident((2 * n1, n1))],
        out_specs=pl.BlockSpec((2 * n1, tn), lambda j: (0, j)),
        out_shape=jax.ShapeDtypeStruct((2 * n1, cols), BF16),
        compiler_params=_cparams(1),
        name="fft1",
    )(x2d, cs1)


def _fft2_kernel(ac_ref, as_ref, tc_ref, ts_ref, rot2_ref, mix_ref, o_ref):
    kb, n2, width = ac_ref.shape
    for kk in range(kb):
        ac = ac_ref[kk].astype(F32)
        as_ = as_ref[kk].astype(F32)
        tc = jnp.broadcast_to(tc_ref[0][:, kk:kk + 1], (n2, width))
        ts = jnp.broadcast_to(ts_ref[0][:, kk:kk + 1], (n2, width))
        b = jnp.concatenate([ac * tc - as_ * ts, as_ * tc + ac * ts], axis=0).astype(BF16)
        p = _dot(rot2_ref[...], b).astype(BF16)
        gd = FOURIER_GROUP_DIM
        stacked = jnp.concatenate(
            [jnp.concatenate([p[:n2, g * gd:(g + 1) * gd], p[n2:, g * gd:(g + 1) * gd]], axis=1)
             for g in range(width // gd)], axis=0)
        mixed = _dot(stacked, mix_ref[...]).astype(BF16)
        for g in range(width // gd):
            lo = kk * width + g * gd
            o_ref[:, lo:lo + gd] = mixed[g * n2:(g + 1) * n2]


def _fft2(a3, tc, ts, cs2, mix, *, kb):
    two_n1, n2, width = a3.shape
    n1 = two_n1 // 2
    nblk = n1 // kb
    return pl.pallas_call(
        _fft2_kernel,
        grid=(nblk,),
        in_specs=[pl.BlockSpec((kb, n2, width), lambda j: (j, 0, 0)),
                  pl.BlockSpec((kb, n2, width), lambda j: (j + nblk, 0, 0)),
                  pl.BlockSpec((1, n2, kb), lambda j: (j, 0, 0)),
                  pl.BlockSpec((1, n2, kb), lambda j: (j, 0, 0)),
                  _resident(cs2.shape), _resident(mix.shape)],
        out_specs=pl.BlockSpec((n2, kb * width), lambda j: (0, j)),
        out_shape=jax.ShapeDtypeStruct((n2, n1 * width), BF16),
        compiler_params=_cparams(1),
        name="fft2",
    )(a3, a3, tc, ts, cs2, mix)


def _dft_tables(s):
    n1 = s // FFT_INNER

    def cos_sin(n, scale):
        ang = 2.0 * np.pi * (np.outer(np.arange(n), np.arange(n)) % n) / n
        return np.cos(ang) * scale, np.sin(ang) * scale

    c1, s1 = cos_sin(n1, n1 ** -0.5)
    c2, s2 = cos_sin(FFT_INNER, FFT_INNER ** -0.5)
    cc, sc = cos_sin(FOURIER_GROUP_DIM, FOURIER_GROUP_DIM ** -0.5)
    as_bf16 = lambda a: jnp.asarray(a, dtype=F32).astype(BF16)
    return (as_bf16(np.concatenate([c1, s1], axis=0)), as_bf16(np.block([[c2, -s2], [s2, c2]])),
            as_bf16(np.concatenate([cc, -sc], axis=0)))


def _twiddles(s, kb):
    n1 = s // FFT_INNER
    prod = (jnp.arange(n1, dtype=jnp.int32)[:, None] * jnp.arange(FFT_INNER, dtype=jnp.int32)[None, :]) % s
    ang = (prod.astype(F32) * (2.0 * math.pi / s)).reshape(n1 // kb, kb, FFT_INNER).transpose(0, 2, 1)
    return jnp.cos(ang), jnp.sin(ang)


def _attn_kernel(q_ref, kt_ref, v_ref, o_ref, *, tk, online):
    tq = q_ref.shape[0]
    qs = jnp.concatenate([q_ref[:, g * LANES:(g + 1) * LANES] for g in range(GQA_GROUP)], axis=0)
    rows = GQA_GROUP * tq
    n_chunks = kt_ref.shape[1] // tk
    acc0 = jnp.zeros((rows, LANES), F32)

    def scores(j):
        off = pl.multiple_of(j * tk, tk)
        return _dot(qs, kt_ref[:, pl.ds(off, tk)]), v_ref[pl.ds(off, tk), :]

    if online:
        def body(j, carry):
            m, acc = carry
            s, v = scores(j)
            m_new = jnp.maximum(m, jnp.max(s, axis=1, keepdims=True))
            p = jnp.exp2(s - m_new).astype(BF16)
            return m_new, acc * jnp.exp2(m - m_new) + _dot(p, v)

        m0 = jnp.full((rows, 1), -jnp.inf, F32)
        _, acc = lax.fori_loop(0, n_chunks, body, (m0, acc0))
    else:
        def body(j, acc):
            s, v = scores(j)
            return acc + _dot(jnp.exp2(s).astype(BF16), v)

        acc = lax.fori_loop(0, n_chunks, body, acc0)
    out = acc[:, :HEAD_DIM] / acc[:, HEAD_DIM:HEAD_DIM + 1]
    o_ref[...] = jnp.concatenate([out[g * tq:(g + 1) * tq] for g in range(GQA_GROUP)],
                                 axis=1).astype(BF16)


def _attn(q, kt, v, *, tq, tk, online):
    s = q.shape[0]
    gw = GQA_GROUP * HEAD_DIM
    return pl.pallas_call(
        functools.partial(_attn_kernel, tk=tk, online=online),
        grid=(N_KV_HEADS, s // tq),
        in_specs=[pl.BlockSpec((tq, GQA_GROUP * LANES), lambda h, i: (i, h)),
                  pl.BlockSpec((LANES, s), lambda h, i: (h, 0)),
                  pl.BlockSpec((s, LANES), lambda h, i: (0, h))],
        out_specs=pl.BlockSpec((tq, gw), lambda h, i: (i, h)),
        out_shape=jax.ShapeDtypeStruct((s, N_KV_HEADS * gw), BF16),
        compiler_params=_cparams(2),
        name="attn_online" if online else "attn",
    )(q, kt, v)


def _memkv_kernel(mem_ref, g_ref, wk_ref, wv_ref, ktm_ref, vm_ref):
    mn = _rms(mem_ref[...], g_ref[...]).astype(BF16)
    kt = _dot(mn, wk_ref[...]).T
    v = _dot(mn, wv_ref[...])
    feat_row = lax.broadcasted_iota(jnp.int32, kt.shape, 0) // HEAD_DIM
    feat_col = lax.broadcasted_iota(jnp.int32, v.shape, 1) // HEAD_DIM
    for h in range(N_CROSS_HEADS):
        ktm_ref[h] = jnp.where(feat_row == h, kt, 0.0).astype(BF16)
        vm_ref[h] = jnp.where(feat_col == h, v, 0.0).astype(BF16)


def _memkv(mem, g, wk, wv):
    m, d = mem.shape
    cw = wk.shape[1]
    whole = lambda shape: pl.BlockSpec(shape, lambda: (0,) * len(shape))
    return pl.pallas_call(
        _memkv_kernel,
        in_specs=[whole((m, d)), whole((1, d)), whole((d, cw)), whole((d, cw))],
        out_specs=[whole((N_CROSS_HEADS, cw, m)), whole((N_CROSS_HEADS, m, cw))],
        out_shape=[jax.ShapeDtypeStruct((N_CROSS_HEADS, cw, m), BF16),
                   jax.ShapeDtypeStruct((N_CROSS_HEADS, m, cw), BF16)],
        compiler_params=pltpu.CompilerParams(vmem_limit_bytes=VMEM_LIMIT),
        name="memkv",
    )(mem, g, wk, wv)


def _merge_kernel(h_ref, fm_ref, o_ref, sf_ref, sa_ref, wfb_ref, wab_ref, wout_ref, gc_ref,
                  wcq_ref, ktm_ref, vm_ref, wco_ref, out_ref, *, sub_tiles):
    sub = h_ref.shape[0] // sub_tiles
    tiles = [slice(r * sub, (r + 1) * sub) for r in range(sub_tiles)]
    y_f = [_dot(fm_ref[t, :], wfb_ref[...]) for t in tiles]
    y_a = [_dot(o_ref[t, :], wab_ref[...]) for t in tiles]
    merged = [(sf_ref[t, :] * f + sa_ref[t, :] * a).astype(BF16) for t, f, a in zip(tiles, y_f, y_a)]
    h = [h_ref[t, :] + _dot(m, wout_ref[...]) for t, m in zip(tiles, merged)]
    n = [_rms(x, gc_ref[...]).astype(BF16) for x in h]
    qc = [(_dot(x, wcq_ref[...]) * (1.0 / math.sqrt(HEAD_DIM))).astype(BF16) for x in n]
    oc = [None] * sub_tiles
    for hh in range(N_CROSS_HEADS):
        s = [_dot(x, ktm_ref[hh]) for x in qc]
        e = [jnp.exp(x - jnp.max(x, axis=1, keepdims=True)) for x in s]
        p = [(x / jnp.sum(x, axis=1, keepdims=True)).astype(BF16) for x in e]
        head = [_dot(x, vm_ref[hh]) for x in p]
        oc = [x if acc is None else acc + x for acc, x in zip(oc, head)]
    for t, x, o in zip(tiles, h, oc):
        out_ref[t, :] = x + _dot(o.astype(BF16), wco_ref[...])


def _merge(h, fm, o, sf, sa, wfb, wab, wout, gc, wcq, ktm, vm, wco, *, tm, sub_tiles):
    s, d = h.shape

    def rows(w):
        return pl.BlockSpec((tm, w), lambda i: (i, 0))

    weights = [wfb, wab, wout, gc, wcq, ktm, vm, wco]
    return pl.pallas_call(
        functools.partial(_merge_kernel, sub_tiles=sub_tiles),
        grid=(s // tm,),
        in_specs=[rows(d), rows(fm.shape[1]), rows(o.shape[1]), rows(d), rows(d)]
                 + [_resident(w.shape) for w in weights],
        out_specs=rows(d),
        out_shape=jax.ShapeDtypeStruct((s, d), F32),
        compiler_params=_cparams(1),
        name="merge",
    )(h, fm, o, sf, sa, *weights)


def _rope_tables(s):
    lane = np.arange(LANES)
    inv = 1.0 / (ROPE_THETA ** (jnp.arange(0, ROPE_AXIS_DIM, 2, dtype=F32) / ROPE_AXIS_DIM))
    inv_lane = inv[lane % ROPE_HALF]
    row_lane = jnp.asarray((lane % HEAD_DIM) < ROPE_AXIS_DIM)
    sign = jnp.asarray(np.where((lane % ROPE_AXIS_DIM) < ROPE_HALF, -1.0, 1.0), dtype=F32)
    ang_r = jnp.arange(s // GRID_W, dtype=F32)[:, None] * inv_lane[None, :]
    ang_c = jnp.arange(GRID_W, dtype=F32)[:, None] * inv_lane[None, :]
    return (jnp.where(row_lane, jnp.cos(ang_r), 0.0), jnp.where(row_lane, jnp.sin(ang_r) * sign, 0.0),
            jnp.where(row_lane, 0.0, jnp.cos(ang_c)), jnp.where(row_lane, 0.0, jnp.sin(ang_c) * sign))


def _pick_tile(s, want):
    t = min(want, s)
    while s % t:
        t //= 2
    return t


def _tiles(s, fft_cols):
    tm = _pick_tile(s, 512)
    return dict(
        tokens=_pick_tile(s, 2 * tm),
        ffn_tokens=_pick_tile(s, 2 * tm),
        ffn_hidden=3 * MXU_WIDTH,
        merge_tokens=_pick_tile(s, 2 * tm),
        merge_sub_tiles=4,
        fft_cols=_pick_tile(fft_cols, 8192),
        fft_k1=8,
        attn_q=_pick_tile(s, 512),
        attn_kv=_pick_tile(s, 4096),
        attn_online_q=_pick_tile(s, 128),
        attn_online_kv=_pick_tile(s, 512),
    )


def kernel(x, mem, g_ffn1, w1_gate, w1_up, w1_down, g_mix, w_in, q_gain, k_gain, w_fourier_branch, w_attn_branch, w_out, g_cross, g_mem, w_cq, w_ck, w_cv, w_co, g_ffn2, w2_gate, w2_up, w2_down, g_final):
    batch, s, d = x.shape
    depth = w_in.shape[0]
    qw = w_attn_branch.shape[1]
    kw = N_KV_HEADS * HEAD_DIM
    fw = w_fourier_branch.shape[1]
    vw = N_KV_HEADS * LANES
    assert s % (FFT_INNER * 8) == 0 and s % GRID_W == 0
    assert qw == N_KV_HEADS * GQA_GROUP * HEAD_DIM and fw == FOURIER_GROUPS * FOURIER_GROUP_DIM
    n1 = s // FFT_INNER
    tiles = _tiles(s, FFT_INNER * fw)
    row = lambda g: g.reshape(1, -1).astype(F32)
    bf = lambda w: w.astype(BF16)

    rope = _rope_tables(s)
    cs1, cs2, mix = _dft_tables(s)
    tw_c, tw_s = _twiddles(s, tiles["fft_k1"])
    lane = np.arange(LANES)
    mxu_lane = np.arange(MXU_WIDTH)
    head_mean = jnp.asarray((mxu_lane[:, None] // HEAD_DIM == mxu_lane[None, :] // HEAD_DIM) / HEAD_DIM,
                            dtype=F32).astype(BF16)
    vone = jnp.asarray((np.arange(vw) % LANES == HEAD_DIM).astype(np.float32)).reshape(1, vw)

    outs = []
    for b in range(batch):
        h = x[b]
        for l in range(depth):
            h = _ffn(h, row(g_ffn1[l]), bf(w1_gate[l]), bf(w1_up[l]), bf(w1_down[l]),
                     row(g_final), final_norm=False, tm=tiles["ffn_tokens"], dff_chunk=tiles["ffn_hidden"])

            wv = bf(w_in[l][:, qw + kw:qw + 2 * kw]).reshape(d, N_KV_HEADS, HEAD_DIM)
            wv = jnp.pad(wv, ((0, 0), (0, 0), (0, LANES - HEAD_DIM))).reshape(d, vw)
            shift = (LOG2E * math.sqrt(HEAD_DIM)) * jnp.max(jnp.abs(q_gain[l])) * jnp.max(jnp.abs(k_gain[l]))
            extra_lane = jnp.asarray((lane == HEAD_DIM).astype(np.float32)).reshape(1, LANES)
            q, kt, v, f, sf, sa = _inproj(
                h, row(g_mix[l]), bf(w_in[l]), wv, row(jnp.tile(q_gain[l], LANES // HEAD_DIM)),
                row(jnp.tile(k_gain[l], LANES // HEAD_DIM)), rope, head_mean,
                extra_lane, -shift * extra_lane, vone, widths=(qw, kw, fw, d), tm=tiles["tokens"])

            a = _fft1(f.reshape(n1, FFT_INNER * fw), cs1, tn=tiles["fft_cols"])
            fm = _fft2(a.reshape(2 * n1, FFT_INNER, fw), tw_c, tw_s, cs2, mix, kb=tiles["fft_k1"])
            fm = fm.reshape(s, fw)

            o = lax.cond(
                shift <= MAX_SAFE_SHIFT,
                functools.partial(_attn, tq=tiles["attn_q"], tk=tiles["attn_kv"], online=False),
                functools.partial(_attn, tq=tiles["attn_online_q"], tk=tiles["attn_online_kv"],
                                  online=True),
                q, kt, v)

            ktm, vm = _memkv(mem[b], row(g_mem[l]), bf(w_ck[l]), bf(w_cv[l]))
            h = _merge(h, fm, o, sf, sa, bf(w_fourier_branch[l]), bf(w_attn_branch[l]), bf(w_out[l]),
                       row(g_cross[l]), bf(w_cq[l]), ktm, vm, bf(w_co[l]),
                       tm=tiles["merge_tokens"], sub_tiles=tiles["merge_sub_tiles"])

            h = _ffn(h, row(g_ffn2[l]), bf(w2_gate[l]), bf(w2_up[l]), bf(w2_down[l]),
                     row(g_final), final_norm=(l == depth - 1), tm=tiles["ffn_tokens"],
                     dff_chunk=tiles["ffn_hidden"])
        outs.append(h)
    return jnp.stack(outs, axis=0)
```

```python
import functools
import math

import numpy as np
import jax
import jax.numpy as jnp
from jax import lax
from jax.experimental import pallas as pl
from jax.experimental.pallas import tpu as pltpu

F32 = jnp.float32
BF16 = jnp.bfloat16

LANES = 128
MXU_WIDTH = 256
GRID_W = 64
HEAD_DIM = 64
N_KV_HEADS = 4
GQA_GROUP = 4
N_CROSS_HEADS = 4
FOURIER_GROUPS = 4
FOURIER_GROUP_DIM = 128
ROPE_AXIS_DIM = HEAD_DIM // 2
ROPE_HALF = ROPE_AXIS_DIM // 2
ROPE_THETA = 10000.0
RMS_EPS = 1e-6
FFT_INNER = 128
VMEM_LIMIT = 56 * 1024 * 1024
LOG2E = math.log2(math.e)
MAX_SAFE_SHIFT = 40.0


def _cparams(n_axes):
    return pltpu.CompilerParams(dimension_semantics=("arbitrary",) * n_axes,
                                vmem_limit_bytes=VMEM_LIMIT)


def _resident(shape):
    zeros = (0,) * len(shape)
    return pl.BlockSpec(shape, lambda *_: zeros, pipeline_mode=pl.Buffered(1))


def _dot(a, b):
    return jnp.dot(a, b, preferred_element_type=F32)


def _rms(x, g):
    y = x * lax.rsqrt(jnp.mean(x * x, axis=-1, keepdims=True) + RMS_EPS)
    return y * g


def _ffn_kernel(x_ref, g_ref, wg_ref, wu_ref, wd_ref, gfin_ref, o_ref, *, final_norm, dff_chunk):
    x = x_ref[...]
    xn = _rms(x, g_ref[...]).astype(BF16)
    dff = wg_ref.shape[1]
    acc = None
    for lo in range(0, dff, dff_chunk):
        hi = min(lo + dff_chunk, dff)
        gate = _dot(xn, wg_ref[:, lo:hi])
        up = _dot(xn, wu_ref[:, lo:hi])
        act = (gate * jax.nn.sigmoid(gate) * up).astype(BF16)
        part = _dot(act, wd_ref[lo:hi, :])
        acc = part if acc is None else acc + part
    h = x + 0.5 * acc
    if final_norm:
        h = _rms(h, gfin_ref[...])
    o_ref[...] = h


def _ffn(x, g, wg, wu, wd, gfin, *, final_norm, tm, dff_chunk):
    s, d = x.shape
    dff = wg.shape[1]
    row = pl.BlockSpec((tm, d), lambda i: (i, 0))
    return pl.pallas_call(
        functools.partial(_ffn_kernel, final_norm=final_norm, dff_chunk=dff_chunk),
        grid=(s // tm,),
        in_specs=[row, _resident((1, d)), _resident((d, dff)), _resident((d, dff)),
                  _resident((dff, d)), _resident((1, d))],
        out_specs=row,
        out_shape=jax.ShapeDtypeStruct((s, d), F32),
        compiler_params=_cparams(1),
        name="ffn_final" if final_norm else "ffn",
    )(x, g, wg, wu, wd, gfin)


def _head_norm_rope(x, gain, head_mean, cos, sin, first_half):
    ms = _dot((x * x).astype(BF16), head_mean)
    halves = []
    for c in range(x.shape[1] // LANES):
        sl = slice(c * LANES, (c + 1) * LANES)
        y = x[:, sl] * lax.rsqrt(ms[:, sl] + RMS_EPS) * gain
        partner = jnp.where(first_half,
                            pltpu.roll(y, LANES - ROPE_HALF, 1),
                            pltpu.roll(y, ROPE_HALF, 1))
        halves.append(y * cos + partner * sin)
    return halves


def _head_slots(xc, extra, low):
    return jnp.where(low, xc, extra), jnp.where(low, pltpu.roll(xc, HEAD_DIM, 1), extra)


def _token_table(row_ref, col_ref):
    col = col_ref[...]
    return jnp.concatenate([jnp.broadcast_to(row_ref[r:r + 1, :], col.shape) + col
                            for r in range(row_ref.shape[0])], axis=0)


def _inproj_kernel(h_ref, g_ref, win_ref, wv_ref, qg_ref, kg_ref, cosr_ref, sinr_ref, cosc_ref,
                   sinc_ref, hm_ref, qx_ref, kx_ref, vone_ref, q_ref, kt_ref, v_ref, f_ref, sf_ref,
                   sa_ref, *, widths):
    qw, kw, fw, gw = widths
    n = _rms(h_ref[...], g_ref[...]).astype(BF16)
    cos = _token_table(cosr_ref, cosc_ref)
    sin = _token_table(sinr_ref, sinc_ref)
    head_mean = hm_ref[...]
    lane = lax.broadcasted_iota(jnp.int32, cos.shape, 1)
    first_half = (lane % ROPE_AXIS_DIM) < ROPE_HALF
    low = lane < HEAD_DIM

    def proj(lo, width):
        return _dot(n, win_ref[:, lo:lo + width])

    q = proj(0, qw)
    pair = head_mean.shape[0]
    slot_lo = 0
    for c in range(qw // pair):
        for qc in _head_norm_rope(q[:, c * pair:(c + 1) * pair], qg_ref[...], head_mean, cos, sin,
                                  first_half):
            for slot in _head_slots(qc * (LOG2E / math.sqrt(HEAD_DIM)), qx_ref[...], low):
                q_ref[:, slot_lo:slot_lo + LANES] = slot.astype(BF16)
                slot_lo += LANES
    k = proj(qw, kw)
    k_slots = []
    for c in range(kw // pair):
        for kc in _head_norm_rope(k[:, c * pair:(c + 1) * pair], kg_ref[...], head_mean, cos, sin,
                                  first_half):
            k_slots.extend(_head_slots(kc, kx_ref[...], low))
    kt_ref[...] = jnp.concatenate(k_slots, axis=1).T.astype(BF16)
    v_ref[...] = (_dot(n, wv_ref[...]) + vone_ref[...]).astype(BF16)
    lo = qw + 2 * kw
    f_ref[...] = proj(lo, fw).astype(BF16)
    lo += fw
    sf_ref[...] = jax.nn.sigmoid(proj(lo, gw)).astype(sf_ref.dtype)
    lo += gw
    sa_ref[...] = jax.nn.sigmoid(proj(lo, gw)).astype(sa_ref.dtype)


def _inproj(h, g, win, wv, qg, kg, rope, head_mean, qx, kx, vone, *, widths, tm):
    s, d = h.shape
    qw, kw, fw, gw = widths
    vw = wv.shape[1]
    slots = LANES // HEAD_DIM
    assert tm % GRID_W == 0

    def rows(w):
        return pl.BlockSpec((tm, w), lambda i: (i, 0))

    lane_row = _resident((1, LANES))
    grid_rows = pl.BlockSpec((tm // GRID_W, LANES), lambda i: (i, 0))
    grid_cols = _resident((GRID_W, LANES))
    return pl.pallas_call(
        functools.partial(_inproj_kernel, widths=widths),
        grid=(s // tm,),
        in_specs=[rows(d), _resident((1, d)), _resident(win.shape), _resident(wv.shape), lane_row,
                  lane_row, grid_rows, grid_rows, grid_cols, grid_cols,
                  _resident(head_mean.shape), lane_row, lane_row, _resident((1, vw))],
        out_specs=[rows(qw * slots), pl.BlockSpec((kw * slots, tm), lambda i: (0, i)), rows(vw),
                   rows(fw), rows(gw), rows(gw)],
        out_shape=[jax.ShapeDtypeStruct((s, qw * slots), BF16),
                   jax.ShapeDtypeStruct((kw * slots, s), BF16),
                   jax.ShapeDtypeStruct((s, vw), BF16), jax.ShapeDtypeStruct((s, fw), BF16),
                   jax.ShapeDtypeStruct((s, gw), BF16), jax.ShapeDtypeStruct((s, gw), BF16)],
        compiler_params=_cparams(1),
        name="inproj",
    )(h, g, win, wv, qg, kg, *rope, head_mean, qx, kx, vone)


def _fft1_kernel(x_ref, cs1_ref, a_ref):
    a_ref[...] = _dot(cs1_ref[...], x_ref[...]).astype(a_ref.dtype)


def _fft1(x2d, cs1, *, tn):
    n1, cols = x2d.shape
    return pl.pallas_call(
        _fft1_kernel,
        grid=(cols // tn,),
        in_specs=[pl.BlockSpec((n1, tn), lambda j: (0, j)), _resident((2 * n1, n1))],
        out_specs=pl.BlockSpec((2 * n1, tn), lambda j: (0, j)),
        out_shape=jax.ShapeDtypeStruct((2 * n1, cols), BF16),
        compiler_params=_cparams(1),
        name="fft1",
    )(x2d, cs1)


def _fft2_kernel(ac_ref, as_ref, tc_ref, ts_ref, rot2_ref, mix_ref, o_ref):
    kb, n2, width = ac_ref.shape
    for kk in range(kb):
        ac = ac_ref[kk].astype(F32)
        as_ = as_ref[kk].astype(F32)
        tc = jnp.broadcast_to(tc_ref[0][:, kk:kk + 1], (n2, width))
        ts = jnp.broadcast_to(ts_ref[0][:, kk:kk + 1], (n2, width))
        b = jnp.concatenate([ac * tc - as_ * ts, as_ * tc + ac * ts], axis=0).astype(BF16)
        p = _dot(rot2_ref[...], b).astype(BF16)
        gd = FOURIER_GROUP_DIM
        stacked = jnp.concatenate(
            [jnp.concatenate([p[:n2, g * gd:(g + 1) * gd], p[n2:, g * gd:(g + 1) * gd]], axis=1)
             for g in range(width // gd)], axis=0)
        mixed = _dot(stacked, mix_ref[...]).astype(BF16)
        for g in range(width // gd):
            lo = kk * width + g * gd
            o_ref[:, lo:lo + gd] = mixed[g * n2:(g + 1) * n2]


def _fft2(a3, tc, ts, cs2, mix, *, kb):
    two_n1, n2, width = a3.shape
    n1 = two_n1 // 2
    nblk = n1 // kb
    return pl.pallas_call(
        _fft2_kernel,
        grid=(nblk,),
        in_specs=[pl.BlockSpec((kb, n2, width), lambda j: (j, 0, 0)),
                  pl.BlockSpec((kb, n2, width), lambda j: (j + nblk, 0, 0)),
                  pl.BlockSpec((1, n2, kb), lambda j: (j, 0, 0)),
                  pl.BlockSpec((1, n2, kb), lambda j: (j, 0, 0)),
                  _resident(cs2.shape), _resident(mix.shape)],
        out_specs=pl.BlockSpec((n2, kb * width), lambda j: (0, j)),
        out_shape=jax.ShapeDtypeStruct((n2, n1 * width), BF16),
        compiler_params=_cparams(1),
        name="fft2",
    )(a3, a3, tc, ts, cs2, mix)


def _dft_tables(s):
    n1 = s // FFT_INNER

    def cos_sin(n, scale):
        ang = 2.0 * np.pi * (np.outer(np.arange(n), np.arange(n)) % n) / n
        return np.cos(ang) * scale, np.sin(ang) * scale

    c1, s1 = cos_sin(n1, n1 ** -0.5)
    c2, s2 = cos_sin(FFT_INNER, FFT_INNER ** -0.5)
    cc, sc = cos_sin(FOURIER_GROUP_DIM, FOURIER_GROUP_DIM ** -0.5)
    as_bf16 = lambda a: jnp.asarray(a, dtype=F32).astype(BF16)
    return (as_bf16(np.concatenate([c1, s1], axis=0)), as_bf16(np.block([[c2, -s2], [s2, c2]])),
            as_bf16(np.concatenate([cc, -sc], axis=0)))


def _twiddles(s, kb):
    n1 = s // FFT_INNER
    ang = 2.0 * np.pi * (np.outer(np.arange(n1), np.arange(FFT_INNER)) % s) / s
    ang = ang.reshape(n1 // kb, kb, FFT_INNER).transpose(0, 2, 1)
    return jnp.asarray(np.cos(ang), dtype=F32), jnp.asarray(np.sin(ang), dtype=F32)


def _attn_kernel(q_ref, kt_ref, v_ref, o_ref, *, tk, online):
    tq = q_ref.shape[0]
    qs = jnp.concatenate([q_ref[:, g * LANES:(g + 1) * LANES] for g in range(GQA_GROUP)], axis=0)
    rows = GQA_GROUP * tq
    n_chunks = kt_ref.shape[1] // tk
    acc0 = jnp.zeros((rows, LANES), F32)

    def scores(j):
        off = pl.multiple_of(j * tk, tk)
        return _dot(qs, kt_ref[:, pl.ds(off, tk)]), v_ref[pl.ds(off, tk), :]

    if online:
        def body(j, carry):
            m, acc = carry
            s, v = scores(j)
            m_new = jnp.maximum(m, jnp.max(s, axis=1, keepdims=True))
            p = jnp.exp2(s - m_new).astype(BF16)
            return m_new, acc * jnp.exp2(m - m_new) + _dot(p, v)

        m0 = jnp.full((rows, 1), -jnp.inf, F32)
        _, acc = lax.fori_loop(0, n_chunks, body, (m0, acc0))
    else:
        def body(j, acc):
            s, v = scores(j)
            return acc + _dot(jnp.exp2(s).astype(BF16), v)

        acc = lax.fori_loop(0, n_chunks, body, acc0)
    out = acc[:, :HEAD_DIM] / acc[:, HEAD_DIM:HEAD_DIM + 1]
    o_ref[...] = jnp.concatenate([out[g * tq:(g + 1) * tq] for g in range(GQA_GROUP)],
                                 axis=1).astype(BF16)


def _attn(q, kt, v, *, tq, tk, online):
    s = q.shape[0]
    gw = GQA_GROUP * HEAD_DIM
    return pl.pallas_call(
        functools.partial(_attn_kernel, tk=tk, online=online),
        grid=(N_KV_HEADS, s // tq),
        in_specs=[pl.BlockSpec((tq, GQA_GROUP * LANES), lambda h, i: (i, h)),
                  pl.BlockSpec((LANES, s), lambda h, i: (h, 0)),
                  pl.BlockSpec((s, LANES), lambda h, i: (0, h))],
        out_specs=pl.BlockSpec((tq, gw), lambda h, i: (i, h)),
        out_shape=jax.ShapeDtypeStruct((s, N_KV_HEADS * gw), BF16),
        compiler_params=_cparams(2),
        name="attn_online" if online else "attn",
    )(q, kt, v)


def _memkv_kernel(mem_ref, g_ref, wk_ref, wv_ref, ktm_ref, vm_ref):
    mn = _rms(mem_ref[...], g_ref[...]).astype(BF16)
    kt = _dot(mn, wk_ref[...]).T
    v = _dot(mn, wv_ref[...])
    feat_row = lax.broadcasted_iota(jnp.int32, kt.shape, 0) // HEAD_DIM
    feat_col = lax.broadcasted_iota(jnp.int32, v.shape, 1) // HEAD_DIM
    for h in range(N_CROSS_HEADS):
        ktm_ref[h] = jnp.where(feat_row == h, kt, 0.0).astype(BF16)
        vm_ref[h] = jnp.where(feat_col == h, v, 0.0).astype(BF16)


def _memkv(mem, g, wk, wv):
    m, d = mem.shape
    cw = wk.shape[1]
    whole = lambda shape: pl.BlockSpec(shape, lambda: (0,) * len(shape))
    return pl.pallas_call(
        _memkv_kernel,
        in_specs=[whole((m, d)), whole((1, d)), whole((d, cw)), whole((d, cw))],
        out_specs=[whole((N_CROSS_HEADS, cw, m)), whole((N_CROSS_HEADS, m, cw))],
        out_shape=[jax.ShapeDtypeStruct((N_CROSS_HEADS, cw, m), BF16),
                   jax.ShapeDtypeStruct((N_CROSS_HEADS, m, cw), BF16)],
        compiler_params=pltpu.CompilerParams(vmem_limit_bytes=VMEM_LIMIT),
        name="memkv",
    )(mem, g, wk, wv)


def _merge_kernel(h_ref, fm_ref, o_ref, sf_ref, sa_ref, wfb_ref, wab_ref, wout_ref, gc_ref,
                  wcq_ref, ktm_ref, vm_ref, wco_ref, out_ref, *, sub_tiles):
    sub = h_ref.shape[0] // sub_tiles
    tiles = [slice(r * sub, (r + 1) * sub) for r in range(sub_tiles)]
    y_f = [_dot(fm_ref[t, :], wfb_ref[...]) for t in tiles]
    y_a = [_dot(o_ref[t, :], wab_ref[...]) for t in tiles]
    merged = [(sf_ref[t, :] * f + sa_ref[t, :] * a).astype(BF16) for t, f, a in zip(tiles, y_f, y_a)]
    h = [h_ref[t, :] + _dot(m, wout_ref[...]) for t, m in zip(tiles, merged)]
    n = [_rms(x, gc_ref[...]).astype(BF16) for x in h]
    qc = [(_dot(x, wcq_ref[...]) * (1.0 / math.sqrt(HEAD_DIM))).astype(BF16) for x in n]
    oc = [None] * sub_tiles
    for hh in range(N_CROSS_HEADS):
        s = [_dot(x, ktm_ref[hh]) for x in qc]
        e = [jnp.exp(x - jnp.max(x, axis=1, keepdims=True)) for x in s]
        p = [(x / jnp.sum(x, axis=1, keepdims=True)).astype(BF16) for x in e]
        head = [_dot(x, vm_ref[hh]) for x in p]
        oc = [x if acc is None else acc + x for acc, x in zip(oc, head)]
    for t, x, o in zip(tiles, h, oc):
        out_ref[t, :] = x + _dot(o.astype(BF16), wco_ref[...])


def _merge(h, fm, o, sf, sa, wfb, wab, wout, gc, wcq, ktm, vm, wco, *, tm, sub_tiles):
    s, d = h.shape

    def rows(w):
        return pl.BlockSpec((tm, w), lambda i: (i, 0))

    weights = [wfb, wab, wout, gc, wcq, ktm, vm, wco]
    return pl.pallas_call(
        functools.partial(_merge_kernel, sub_tiles=sub_tiles),
        grid=(s // tm,),
        in_specs=[rows(d), rows(fm.shape[1]), rows(o.shape[1]), rows(d), rows(d)]
                 + [_resident(w.shape) for w in weights],
        out_specs=rows(d),
        out_shape=jax.ShapeDtypeStruct((s, d), F32),
        compiler_params=_cparams(1),
        name="merge",
    )(h, fm, o, sf, sa, *weights)


def _rope_tables(s):
    f32 = np.float32
    lane = np.arange(LANES)
    inv = f32(1.0) / (f32(ROPE_THETA) ** (np.arange(0, ROPE_AXIS_DIM, 2, dtype=f32) / f32(ROPE_AXIS_DIM)))
    inv_lane = inv[lane % ROPE_HALF]
    row_lane = (lane % HEAD_DIM) < ROPE_AXIS_DIM
    sign = np.where((lane % ROPE_AXIS_DIM) < ROPE_HALF, f32(-1.0), f32(1.0))
    ang_r = np.arange(s // GRID_W, dtype=f32)[:, None] * inv_lane[None, :]
    ang_c = np.arange(GRID_W, dtype=f32)[:, None] * inv_lane[None, :]
    tables = (np.where(row_lane, np.cos(ang_r), 0.0), np.where(row_lane, np.sin(ang_r) * sign, 0.0),
              np.where(row_lane, 0.0, np.cos(ang_c)), np.where(row_lane, 0.0, np.sin(ang_c) * sign))
    return tuple(jnp.asarray(t, dtype=F32) for t in tables)


def _pick_tile(s, want):
    t = min(want, s)
    while s % t:
        t //= 2
    return t


def _tiles(s, fft_cols):
    tm = _pick_tile(s, 512)
    return dict(
        tokens=tm,
        ffn_tokens=_pick_tile(s, 2 * tm),
        ffn_hidden=3 * MXU_WIDTH,
        merge_tokens=_pick_tile(s, 2 * tm),
        merge_sub_tiles=4,
        fft_cols=_pick_tile(fft_cols, 8192),
        fft_k1=8,
        attn_q=_pick_tile(s, 512),
        attn_kv=_pick_tile(s, 4096),
        attn_online_q=_pick_tile(s, 128),
        attn_online_kv=_pick_tile(s, 512),
    )


def kernel(x, mem, g_ffn1, w1_gate, w1_up, w1_down, g_mix, w_in, q_gain, k_gain, w_fourier_branch, w_attn_branch, w_out, g_cross, g_mem, w_cq, w_ck, w_cv, w_co, g_ffn2, w2_gate, w2_up, w2_down, g_final):
    batch, s, d = x.shape
    depth = w_in.shape[0]
    qw = w_attn_branch.shape[1]
    kw = N_KV_HEADS * HEAD_DIM
    fw = w_fourier_branch.shape[1]
    vw = N_KV_HEADS * LANES
    assert s % (FFT_INNER * 8) == 0 and s % GRID_W == 0
    assert qw == N_KV_HEADS * GQA_GROUP * HEAD_DIM and fw == FOURIER_GROUPS * FOURIER_GROUP_DIM
    n1 = s // FFT_INNER
    tiles = _tiles(s, FFT_INNER * fw)
    row = lambda g: g.reshape(1, -1).astype(F32)
    bf = lambda w: w.astype(BF16)

    rope = _rope_tables(s)
    cs1, cs2, mix = _dft_tables(s)
    tw_c, tw_s = _twiddles(s, tiles["fft_k1"])
    lane = np.arange(LANES)
    mxu_lane = np.arange(MXU_WIDTH)
    head_mean = jnp.asarray((mxu_lane[:, None] // HEAD_DIM == mxu_lane[None, :] // HEAD_DIM) / HEAD_DIM,
                            dtype=F32).astype(BF16)
    vone = jnp.asarray((np.arange(vw) % LANES == HEAD_DIM).astype(np.float32)).reshape(1, vw)
    v_col = np.arange(kw)
    spread = np.zeros((kw, vw), np.float32)
    spread[v_col, v_col // HEAD_DIM * LANES + v_col % HEAD_DIM] = 1.0
    spread = jnp.asarray(spread)

    outs = []
    for b in range(batch):
        h = x[b]
        for l in range(depth):
            h = _ffn(h, row(g_ffn1[l]), bf(w1_gate[l]), bf(w1_up[l]), bf(w1_down[l]),
                     row(g_final), final_norm=False, tm=tiles["ffn_tokens"], dff_chunk=tiles["ffn_hidden"])

            wv = bf(jnp.dot(w_in[l][:, qw + kw:qw + 2 * kw], spread))
            shift = (LOG2E * math.sqrt(HEAD_DIM)) * jnp.max(jnp.abs(q_gain[l])) * jnp.max(jnp.abs(k_gain[l]))
            extra_lane = jnp.asarray((lane == HEAD_DIM).astype(np.float32)).reshape(1, LANES)
            q, kt, v, f, sf, sa = _inproj(
                h, row(g_mix[l]), bf(w_in[l]), wv, row(jnp.tile(q_gain[l], LANES // HEAD_DIM)),
                row(jnp.tile(k_gain[l], LANES // HEAD_DIM)), rope, head_mean,
                extra_lane, -shift * extra_lane, vone, widths=(qw, kw, fw, d), tm=tiles["tokens"])

            a = _fft1(f.reshape(n1, FFT_INNER * fw), cs1, tn=tiles["fft_cols"])
            fm = _fft2(a.reshape(2 * n1, FFT_INNER, fw), tw_c, tw_s, cs2, mix, kb=tiles["fft_k1"])
            fm = fm.reshape(s, fw)

            o = lax.cond(
                shift <= MAX_SAFE_SHIFT,
                functools.partial(_attn, tq=tiles["attn_q"], tk=tiles["attn_kv"], online=False),
                functools.partial(_attn, tq=tiles["attn_online_q"], tk=tiles["attn_online_kv"],
                                  online=True),
                q, kt, v)

            ktm, vm = _memkv(mem[b], row(g_mem[l]), bf(w_ck[l]), bf(w_cv[l]))
            h = _merge(h, fm, o, sf, sa, bf(w_fourier_branch[l]), bf(w_attn_branch[l]), bf(w_out[l]),
                       row(g_cross[l]), bf(w_cq[l]), ktm, vm, bf(w_co[l]),
                       tm=tiles["merge_tokens"], sub_tiles=tiles["merge_sub_tiles"])

            h = _ffn(h, row(g_ffn2[l]), bf(w2_gate[l]), bf(w2_up[l]), bf(w2_down[l]),
                     row(g_final), final_norm=(l == depth - 1), tm=tiles["ffn_tokens"],
                     dff_chunk=tiles["ffn_hidden"])
        outs.append(h)
    return jnp.stack(outs, axis=0)
```

```python
import functools
import math

import numpy as np
import jax
import jax.numpy as jnp
from jax import lax
from jax.experimental import pallas as pl
from jax.experimental.pallas import tpu as pltpu

F32 = jnp.float32
BF16 = jnp.bfloat16

LANES = 128
MXU_WIDTH = 256
GRID_W = 64
HEAD_DIM = 64
N_KV_HEADS = 4
GQA_GROUP = 4
N_CROSS_HEADS = 4
FOURIER_GROUPS = 4
FOURIER_GROUP_DIM = 128
ROPE_AXIS_DIM = HEAD_DIM // 2
ROPE_HALF = ROPE_AXIS_DIM // 2
ROPE_THETA = 10000.0
RMS_EPS = 1e-6
FFT_INNER = 128
VMEM_LIMIT = 56 * 1024 * 1024
LOG2E = math.log2(math.e)
MAX_SAFE_SHIFT = 40.0


def _cparams(n_axes):
    return pltpu.CompilerParams(dimension_semantics=("arbitrary",) * n_axes,
                                vmem_limit_bytes=VMEM_LIMIT)


def _resident(shape):
    zeros = (0,) * len(shape)
    return pl.BlockSpec(shape, lambda *_: zeros, pipeline_mode=pl.Buffered(1))


def _dot(a, b):
    return jnp.dot(a, b, preferred_element_type=F32)


def _rms(x, g):
    y = x * lax.rsqrt(jnp.mean(x * x, axis=-1, keepdims=True) + RMS_EPS)
    return y * g


def _ffn_kernel(x_ref, g_ref, wg_ref, wu_ref, wd_ref, gfin_ref, o_ref, *, final_norm, dff_chunk):
    x = x_ref[...]
    xn = _rms(x, g_ref[...]).astype(BF16)
    dff = wg_ref.shape[1]
    acc = None
    for lo in range(0, dff, dff_chunk):
        hi = min(lo + dff_chunk, dff)
        gate = _dot(xn, wg_ref[:, lo:hi])
        up = _dot(xn, wu_ref[:, lo:hi])
        act = (gate * jax.nn.sigmoid(gate) * up).astype(BF16)
        part = _dot(act, wd_ref[lo:hi, :])
        acc = part if acc is None else acc + part
    h = x + 0.5 * acc
    if final_norm:
        h = _rms(h, gfin_ref[...])
    o_ref[...] = h


def _ffn(x, g, wg, wu, wd, gfin, *, final_norm, tm, dff_chunk):
    s, d = x.shape
    dff = wg.shape[1]
    row = pl.BlockSpec((tm, d), lambda i: (i, 0))
    return pl.pallas_call(
        functools.partial(_ffn_kernel, final_norm=final_norm, dff_chunk=dff_chunk),
        grid=(s // tm,),
        in_specs=[row, _resident((1, d)), _resident((d, dff)), _resident((d, dff)),
                  _resident((dff, d)), _resident((1, d))],
        out_specs=row,
        out_shape=jax.ShapeDtypeStruct((s, d), F32),
        compiler_params=_cparams(1),
        name="ffn_final" if final_norm else "ffn",
    )(x, g, wg, wu, wd, gfin)


def _head_norm_rope(x, gain, head_mean, cos, sin, first_half):
    ms = _dot((x * x).astype(BF16), head_mean)
    halves = []
    for c in range(x.shape[1] // LANES):
        sl = slice(c * LANES, (c + 1) * LANES)
        y = x[:, sl] * lax.rsqrt(ms[:, sl] + RMS_EPS) * gain
        partner = jnp.where(first_half,
                            pltpu.roll(y, LANES - ROPE_HALF, 1),
                            pltpu.roll(y, ROPE_HALF, 1))
        halves.append(y * cos + partner * sin)
    return halves


def _head_slots(xc, extra, low):
    return jnp.where(low, xc, extra), jnp.where(low, pltpu.roll(xc, HEAD_DIM, 1), extra)


def _token_table(row_ref, col_ref):
    col = col_ref[...]
    return jnp.concatenate([jnp.broadcast_to(row_ref[r:r + 1, :], col.shape) + col
                            for r in range(row_ref.shape[0])], axis=0)


def _inproj_kernel(h_ref, g_ref, win_ref, wv_ref, qg_ref, kg_ref, cosr_ref, sinr_ref, cosc_ref,
                   sinc_ref, hm_ref, qx_ref, kx_ref, vone_ref, q_ref, kt_ref, v_ref, f_ref, sf_ref,
                   sa_ref, *, widths):
    qw, kw, fw, gw = widths
    n = _rms(h_ref[...], g_ref[...]).astype(BF16)
    cos = _token_table(cosr_ref, cosc_ref)
    sin = _token_table(sinr_ref, sinc_ref)
    head_mean = hm_ref[...]
    lane = lax.broadcasted_iota(jnp.int32, cos.shape, 1)
    first_half = (lane % ROPE_AXIS_DIM) < ROPE_HALF
    low = lane < HEAD_DIM

    def proj(lo, width):
        return _dot(n, win_ref[:, lo:lo + width])

    q = proj(0, qw)
    pair = head_mean.shape[0]
    slot_lo = 0
    for c in range(qw // pair):
        for qc in _head_norm_rope(q[:, c * pair:(c + 1) * pair], qg_ref[...], head_mean, cos, sin,
                                  first_half):
            for slot in _head_slots(qc * (LOG2E / math.sqrt(HEAD_DIM)), qx_ref[...], low):
                q_ref[:, slot_lo:slot_lo + LANES] = slot.astype(BF16)
                slot_lo += LANES
    k = proj(qw, kw)
    k_slots = []
    for c in range(kw // pair):
        for kc in _head_norm_rope(k[:, c * pair:(c + 1) * pair], kg_ref[...], head_mean, cos, sin,
                                  first_half):
            k_slots.extend(_head_slots(kc, kx_ref[...], low))
    kt_ref[...] = jnp.concatenate(k_slots, axis=1).T.astype(BF16)
    v_ref[...] = (_dot(n, wv_ref[...]) + vone_ref[...]).astype(BF16)
    lo = qw + 2 * kw
    f_ref[...] = proj(lo, fw).astype(BF16)
    lo += fw
    sf_ref[...] = jax.nn.sigmoid(proj(lo, gw)).astype(sf_ref.dtype)
    lo += gw
    sa_ref[...] = jax.nn.sigmoid(proj(lo, gw)).astype(sa_ref.dtype)


def _inproj(h, g, win, wv, qg, kg, rope, head_mean, qx, kx, vone, *, widths, tm):
    s, d = h.shape
    qw, kw, fw, gw = widths
    vw = wv.shape[1]
    slots = LANES // HEAD_DIM
    assert tm % GRID_W == 0

    def rows(w):
        return pl.BlockSpec((tm, w), lambda i: (i, 0))

    lane_row = _resident((1, LANES))
    grid_rows = pl.BlockSpec((tm // GRID_W, LANES), lambda i: (i, 0))
    grid_cols = _resident((GRID_W, LANES))
    return pl.pallas_call(
        functools.partial(_inproj_kernel, widths=widths),
        grid=(s // tm,),
        in_specs=[rows(d), _resident((1, d)), _resident(win.shape), _resident(wv.shape), lane_row,
                  lane_row, grid_rows, grid_rows, grid_cols, grid_cols,
                  _resident(head_mean.shape), lane_row, lane_row, _resident((1, vw))],
        out_specs=[rows(qw * slots), pl.BlockSpec((kw * slots, tm), lambda i: (0, i)), rows(vw),
                   rows(fw), rows(gw), rows(gw)],
        out_shape=[jax.ShapeDtypeStruct((s, qw * slots), BF16),
                   jax.ShapeDtypeStruct((kw * slots, s), BF16),
                   jax.ShapeDtypeStruct((s, vw), BF16), jax.ShapeDtypeStruct((s, fw), BF16),
                   jax.ShapeDtypeStruct((s, gw), BF16), jax.ShapeDtypeStruct((s, gw), BF16)],
        compiler_params=_cparams(1),
        name="inproj",
    )(h, g, win, wv, qg, kg, *rope, head_mean, qx, kx, vone)


def _fft1_kernel(x_ref, cs1_ref, a_ref):
    _, nb2, width = x_ref.shape
    for j in range(nb2):
        a_ref[:, j * width:(j + 1) * width] = _dot(cs1_ref[...], x_ref[:, j, :]).astype(a_ref.dtype)


def _fft1(x3, cs1, *, nb2):
    n1, n2, width = x3.shape
    return pl.pallas_call(
        _fft1_kernel,
        grid=(n2 // nb2,),
        in_specs=[pl.BlockSpec((n1, nb2, width), lambda j: (0, j, 0)), _resident((2 * n1, n1))],
        out_specs=pl.BlockSpec((2 * n1, nb2 * width), lambda j: (0, j)),
        out_shape=jax.ShapeDtypeStruct((2 * n1, n2 * width), BF16),
        compiler_params=_cparams(1),
        name="fft1",
    )(x3, cs1)


def _fft2_kernel(ac_ref, as_ref, tc_ref, ts_ref, rot2_ref, mix_ref, o_ref):
    kb, n2, width = ac_ref.shape
    for kk in range(kb):
        ac = ac_ref[kk].astype(F32)
        as_ = as_ref[kk].astype(F32)
        tc = jnp.broadcast_to(tc_ref[0][:, kk:kk + 1], (n2, width))
        ts = jnp.broadcast_to(ts_ref[0][:, kk:kk + 1], (n2, width))
        b = jnp.concatenate([ac * tc - as_ * ts, as_ * tc + ac * ts], axis=0).astype(BF16)
        p = _dot(rot2_ref[...], b).astype(BF16)
        gd = FOURIER_GROUP_DIM
        stacked = jnp.concatenate(
            [jnp.concatenate([p[:n2, g * gd:(g + 1) * gd], p[n2:, g * gd:(g + 1) * gd]], axis=1)
             for g in range(width // gd)], axis=0)
        mixed = _dot(stacked, mix_ref[...])
        for g in range(width // gd):
            o_ref[:, kk, g * gd:(g + 1) * gd] = mixed[g * n2:(g + 1) * n2].astype(o_ref.dtype)


def _fft2(a3, tc, ts, cs2, mix, *, kb):
    two_n1, n2, width = a3.shape
    n1 = two_n1 // 2
    nblk = n1 // kb
    return pl.pallas_call(
        _fft2_kernel,
        grid=(nblk,),
        in_specs=[pl.BlockSpec((kb, n2, width), lambda j: (j, 0, 0)),
                  pl.BlockSpec((kb, n2, width), lambda j: (j + nblk, 0, 0)),
                  pl.BlockSpec((1, n2, kb), lambda j: (j, 0, 0)),
                  pl.BlockSpec((1, n2, kb), lambda j: (j, 0, 0)),
                  _resident(cs2.shape), _resident(mix.shape)],
        out_specs=pl.BlockSpec((n2, kb, width), lambda j: (0, j, 0)),
        out_shape=jax.ShapeDtypeStruct((n2, n1, width), F32),
        compiler_params=_cparams(1),
        name="fft2",
    )(a3, a3, tc, ts, cs2, mix)


def _dft_tables(s):
    n1 = s // FFT_INNER

    def cos_sin(n, scale):
        ang = 2.0 * np.pi * (np.outer(np.arange(n), np.arange(n)) % n) / n
        return np.cos(ang) * scale, np.sin(ang) * scale

    c1, s1 = cos_sin(n1, n1 ** -0.5)
    c2, s2 = cos_sin(FFT_INNER, FFT_INNER ** -0.5)
    cc, sc = cos_sin(FOURIER_GROUP_DIM, FOURIER_GROUP_DIM ** -0.5)
    as_bf16 = lambda a: jnp.asarray(a, dtype=F32).astype(BF16)
    return (as_bf16(np.concatenate([c1, s1], axis=0)), as_bf16(np.block([[c2, -s2], [s2, c2]])),
            as_bf16(np.concatenate([cc, -sc], axis=0)))


def _twiddles(s, kb):
    n1 = s // FFT_INNER
    ang = 2.0 * np.pi * (np.outer(np.arange(n1), np.arange(FFT_INNER)) % s) / s
    ang = ang.reshape(n1 // kb, kb, FFT_INNER).transpose(0, 2, 1)
    return jnp.asarray(np.cos(ang), dtype=F32), jnp.asarray(np.sin(ang), dtype=F32)


def _attn_kernel(q_ref, kt_ref, v_ref, o_ref, *, tk, online):
    tq = q_ref.shape[0]
    qs = jnp.concatenate([q_ref[:, g * LANES:(g + 1) * LANES] for g in range(GQA_GROUP)], axis=0)
    rows = GQA_GROUP * tq
    n_chunks = kt_ref.shape[1] // tk
    acc0 = jnp.zeros((rows, LANES), F32)

    def scores(j):
        off = pl.multiple_of(j * tk, tk)
        return _dot(qs, kt_ref[:, pl.ds(off, tk)]), v_ref[pl.ds(off, tk), :]

    if online:
        def body(j, carry):
            m, acc = carry
            s, v = scores(j)
            m_new = jnp.maximum(m, jnp.max(s, axis=1, keepdims=True))
            p = jnp.exp2(s - m_new).astype(BF16)
            return m_new, acc * jnp.exp2(m - m_new) + _dot(p, v)

        m0 = jnp.full((rows, 1), -jnp.inf, F32)
        _, acc = lax.fori_loop(0, n_chunks, body, (m0, acc0))
    else:
        def body(j, acc):
            s, v = scores(j)
            return acc + _dot(jnp.exp2(s).astype(BF16), v)

        acc = lax.fori_loop(0, n_chunks, body, acc0)
    out = acc[:, :HEAD_DIM] / acc[:, HEAD_DIM:HEAD_DIM + 1]
    o_ref[...] = jnp.concatenate([out[g * tq:(g + 1) * tq] for g in range(GQA_GROUP)],
                                 axis=1).astype(BF16)


def _attn(q, kt, v, *, tq, tk, online):
    s = q.shape[0]
    gw = GQA_GROUP * HEAD_DIM
    return pl.pallas_call(
        functools.partial(_attn_kernel, tk=tk, online=online),
        grid=(N_KV_HEADS, s // tq),
        in_specs=[pl.BlockSpec((tq, GQA_GROUP * LANES), lambda h, i: (i, h)),
                  pl.BlockSpec((LANES, s), lambda h, i: (h, 0)),
                  pl.BlockSpec((s, LANES), lambda h, i: (0, h))],
        out_specs=pl.BlockSpec((tq, gw), lambda h, i: (i, h)),
        out_shape=jax.ShapeDtypeStruct((s, N_KV_HEADS * gw), BF16),
        compiler_params=_cparams(2),
        name="attn_online" if online else "attn",
    )(q, kt, v)


def _memkv_kernel(mem_ref, g_ref, wk_ref, wv_ref, ktm_ref, vm_ref):
    mn = _rms(mem_ref[...], g_ref[...]).astype(BF16)
    kt = _dot(mn, wk_ref[...]).T
    v = _dot(mn, wv_ref[...])
    feat_row = lax.broadcasted_iota(jnp.int32, kt.shape, 0) // HEAD_DIM
    feat_col = lax.broadcasted_iota(jnp.int32, v.shape, 1) // HEAD_DIM
    for h in range(N_CROSS_HEADS):
        ktm_ref[h] = jnp.where(feat_row == h, kt, 0.0).astype(BF16)
        vm_ref[h] = jnp.where(feat_col == h, v, 0.0).astype(BF16)


def _memkv(mem, g, wk, wv):
    m, d = mem.shape
    cw = wk.shape[1]
    whole = lambda shape: pl.BlockSpec(shape, lambda: (0,) * len(shape))
    return pl.pallas_call(
        _memkv_kernel,
        in_specs=[whole((m, d)), whole((1, d)), whole((d, cw)), whole((d, cw))],
        out_specs=[whole((N_CROSS_HEADS, cw, m)), whole((N_CROSS_HEADS, m, cw))],
        out_shape=[jax.ShapeDtypeStruct((N_CROSS_HEADS, cw, m), BF16),
                   jax.ShapeDtypeStruct((N_CROSS_HEADS, m, cw), BF16)],
        compiler_params=pltpu.CompilerParams(vmem_limit_bytes=VMEM_LIMIT),
        name="memkv",
    )(mem, g, wk, wv)


def _merge_kernel(h_ref, fm_ref, o_ref, sf_ref, sa_ref, wfb_ref, wab_ref, wout_ref, gc_ref,
                  wcq_ref, ktm_ref, vm_ref, wco_ref, out_ref, *, sub_tiles):
    sub = h_ref.shape[0] // sub_tiles
    tiles = [slice(r * sub, (r + 1) * sub) for r in range(sub_tiles)]
    y_f = [_dot(fm_ref[t, :].astype(BF16), wfb_ref[...]) for t in tiles]
    y_a = [_dot(o_ref[t, :], wab_ref[...]) for t in tiles]
    merged = [(sf_ref[t, :] * f + sa_ref[t, :] * a).astype(BF16) for t, f, a in zip(tiles, y_f, y_a)]
    h = [h_ref[t, :] + _dot(m, wout_ref[...]) for t, m in zip(tiles, merged)]
    n = [_rms(x, gc_ref[...]).astype(BF16) for x in h]
    qc = [(_dot(x, wcq_ref[...]) * (1.0 / math.sqrt(HEAD_DIM))).astype(BF16) for x in n]
    oc = [None] * sub_tiles
    for hh in range(N_CROSS_HEADS):
        s = [_dot(x, ktm_ref[hh]) for x in qc]
        e = [jnp.exp(x - jnp.max(x, axis=1, keepdims=True)) for x in s]
        p = [(x / jnp.sum(x, axis=1, keepdims=True)).astype(BF16) for x in e]
        head = [_dot(x, vm_ref[hh]) for x in p]
        oc = [x if acc is None else acc + x for acc, x in zip(oc, head)]
    for t, x, o in zip(tiles, h, oc):
        out_ref[t, :] = x + _dot(o.astype(BF16), wco_ref[...])


def _merge(h, fm, o, sf, sa, wfb, wab, wout, gc, wcq, ktm, vm, wco, *, tm, sub_tiles):
    s, d = h.shape

    def rows(w):
        return pl.BlockSpec((tm, w), lambda i: (i, 0))

    weights = [wfb, wab, wout, gc, wcq, ktm, vm, wco]
    return pl.pallas_call(
        functools.partial(_merge_kernel, sub_tiles=sub_tiles),
        grid=(s // tm,),
        in_specs=[rows(d), rows(fm.shape[1]), rows(o.shape[1]), rows(d), rows(d)]
                 + [_resident(w.shape) for w in weights],
        out_specs=rows(d),
        out_shape=jax.ShapeDtypeStruct((s, d), F32),
        compiler_params=_cparams(1),
        name="merge",
    )(h, fm, o, sf, sa, *weights)


def _rope_tables(s):
    f32 = np.float32
    lane = np.arange(LANES)
    inv = f32(1.0) / (f32(ROPE_THETA) ** (np.arange(0, ROPE_AXIS_DIM, 2, dtype=f32) / f32(ROPE_AXIS_DIM)))
    inv_lane = inv[lane % ROPE_HALF]
    row_lane = (lane % HEAD_DIM) < ROPE_AXIS_DIM
    sign = np.where((lane % ROPE_AXIS_DIM) < ROPE_HALF, f32(-1.0), f32(1.0))
    ang_r = np.arange(s // GRID_W, dtype=f32)[:, None] * inv_lane[None, :]
    ang_c = np.arange(GRID_W, dtype=f32)[:, None] * inv_lane[None, :]
    tables = (np.where(row_lane, np.cos(ang_r), 0.0), np.where(row_lane, np.sin(ang_r) * sign, 0.0),
              np.where(row_lane, 0.0, np.cos(ang_c)), np.where(row_lane, 0.0, np.sin(ang_c) * sign))
    return tuple(jnp.asarray(t, dtype=F32) for t in tables)


def _pick_tile(s, want):
    t = min(want, s)
    while s % t:
        t //= 2
    return t


def _tiles(s):
    tm = _pick_tile(s, 512)
    return dict(
        tokens=tm,
        ffn_tokens=_pick_tile(s, 2 * tm),
        ffn_hidden=3 * MXU_WIDTH,
        merge_tokens=_pick_tile(s, 2 * tm),
        merge_sub_tiles=4,
        fft_n2=16,
        fft_k1=8,
        attn_q=_pick_tile(s, 512),
        attn_kv=_pick_tile(s, 4096),
        attn_online_q=_pick_tile(s, 128),
        attn_online_kv=_pick_tile(s, 512),
    )


def kernel(x, mem, g_ffn1, w1_gate, w1_up, w1_down, g_mix, w_in, q_gain, k_gain, w_fourier_branch, w_attn_branch, w_out, g_cross, g_mem, w_cq, w_ck, w_cv, w_co, g_ffn2, w2_gate, w2_up, w2_down, g_final):
    batch, s, d = x.shape
    depth = w_in.shape[0]
    qw = w_attn_branch.shape[1]
    kw = N_KV_HEADS * HEAD_DIM
    fw = w_fourier_branch.shape[1]
    vw = N_KV_HEADS * LANES
    assert s % (FFT_INNER * 8) == 0 and s % GRID_W == 0
    assert qw == N_KV_HEADS * GQA_GROUP * HEAD_DIM and fw == FOURIER_GROUPS * FOURIER_GROUP_DIM
    n1 = s // FFT_INNER
    tiles = _tiles(s)
    row = lambda g: g.reshape(1, -1).astype(F32)
    bf = lambda w: w.astype(BF16)

    rope = _rope_tables(s)
    cs1, cs2, mix = _dft_tables(s)
    tw_c, tw_s = _twiddles(s, tiles["fft_k1"])
    lane = np.arange(LANES)
    mxu_lane = np.arange(MXU_WIDTH)
    head_mean = jnp.asarray((mxu_lane[:, None] // HEAD_DIM == mxu_lane[None, :] // HEAD_DIM) / HEAD_DIM,
                            dtype=F32).astype(BF16)
    vone = jnp.asarray((np.arange(vw) % LANES == HEAD_DIM).astype(np.float32)).reshape(1, vw)
    v_col = np.arange(kw)
    spread = np.zeros((kw, vw), np.float32)
    spread[v_col, v_col // HEAD_DIM * LANES + v_col % HEAD_DIM] = 1.0
    spread = jnp.asarray(spread)

    outs = []
    for b in range(batch):
        h = x[b]
        for l in range(depth):
            h = _ffn(h, row(g_ffn1[l]), bf(w1_gate[l]), bf(w1_up[l]), bf(w1_down[l]),
                     row(g_final), final_norm=False, tm=tiles["ffn_tokens"], dff_chunk=tiles["ffn_hidden"])

            wv = bf(jnp.dot(w_in[l][:, qw + kw:qw + 2 * kw], spread))
            shift = (LOG2E * math.sqrt(HEAD_DIM)) * jnp.max(jnp.abs(q_gain[l])) * jnp.max(jnp.abs(k_gain[l]))
            extra_lane = jnp.asarray((lane == HEAD_DIM).astype(np.float32)).reshape(1, LANES)
            q, kt, v, f, sf, sa = _inproj(
                h, row(g_mix[l]), bf(w_in[l]), wv, row(jnp.tile(q_gain[l], LANES // HEAD_DIM)),
                row(jnp.tile(k_gain[l], LANES // HEAD_DIM)), rope, head_mean,
                extra_lane, -shift * extra_lane, vone, widths=(qw, kw, fw, d), tm=tiles["tokens"])

            a = _fft1(f.reshape(n1, FFT_INNER, fw), cs1, nb2=tiles["fft_n2"])
            fm = _fft2(a.reshape(2 * n1, FFT_INNER, fw), tw_c, tw_s, cs2, mix, kb=tiles["fft_k1"])
            fm = fm.reshape(s, fw)

            o = lax.cond(
                shift <= MAX_SAFE_SHIFT,
                functools.partial(_attn, tq=tiles["attn_q"], tk=tiles["attn_kv"], online=False),
                functools.partial(_attn, tq=tiles["attn_online_q"], tk=tiles["attn_online_kv"],
                                  online=True),
                q, kt, v)

            ktm, vm = _memkv(mem[b], row(g_mem[l]), bf(w_ck[l]), bf(w_cv[l]))
            h = _merge(h, fm, o, sf, sa, bf(w_fourier_branch[l]), bf(w_attn_branch[l]), bf(w_out[l]),
                       row(g_cross[l]), bf(w_cq[l]), ktm, vm, bf(w_co[l]),
                       tm=tiles["merge_tokens"], sub_tiles=tiles["merge_sub_tiles"])

            h = _ffn(h, row(g_ffn2[l]), bf(w2_gate[l]), bf(w2_up[l]), bf(w2_down[l]),
                     row(g_final), final_norm=(l == depth - 1), tm=tiles["ffn_tokens"],
                     dff_chunk=tiles["ffn_hidden"])
        outs.append(h)
    return jnp.stack(outs, axis=0)
```

```python
import functools
import math

import numpy as np
import jax
import jax.numpy as jnp
from jax import lax
from jax.experimental import pallas as pl
from jax.experimental.pallas import tpu as pltpu

F32 = jnp.float32
BF16 = jnp.bfloat16

LANES = 128
MXU_WIDTH = 256
GRID_W = 64
HEAD_DIM = 64
N_KV_HEADS = 4
GQA_GROUP = 4
N_CROSS_HEADS = 4
FOURIER_GROUPS = 4
FOURIER_GROUP_DIM = 128
ROPE_AXIS_DIM = HEAD_DIM // 2
ROPE_HALF = ROPE_AXIS_DIM // 2
ROPE_THETA = 10000.0
RMS_EPS = 1e-6
FFT_INNER = 128
VMEM_LIMIT = 56 * 1024 * 1024
LOG2E = math.log2(math.e)
MAX_SAFE_SHIFT = 40.0


def _cparams(n_axes):
    return pltpu.CompilerParams(dimension_semantics=("arbitrary",) * n_axes,
                                vmem_limit_bytes=VMEM_LIMIT)


def _resident(shape):
    zeros = (0,) * len(shape)
    return pl.BlockSpec(shape, lambda *_: zeros, pipeline_mode=pl.Buffered(1))


def _dot(a, b):
    return jnp.dot(a, b, preferred_element_type=F32)


def _rms(x, g):
    y = x * lax.rsqrt(jnp.mean(x * x, axis=-1, keepdims=True) + RMS_EPS)
    return y * g


def _ffn_kernel(x_ref, g_ref, wg_ref, wu_ref, wd_ref, gfin_ref, o_ref, *, final_norm, dff_chunk):
    x = x_ref[...]
    xn = _rms(x, g_ref[...]).astype(BF16)
    dff = wg_ref.shape[1]
    acc = None
    for lo in range(0, dff, dff_chunk):
        hi = min(lo + dff_chunk, dff)
        gate = _dot(xn, wg_ref[:, lo:hi])
        up = _dot(xn, wu_ref[:, lo:hi])
        act = (gate * jax.nn.sigmoid(gate) * up).astype(BF16)
        part = _dot(act, wd_ref[lo:hi, :])
        acc = part if acc is None else acc + part
    h = x + 0.5 * acc
    if final_norm:
        h = _rms(h, gfin_ref[...])
    o_ref[...] = h


def _ffn(x, g, wg, wu, wd, gfin, *, final_norm, tm, dff_chunk):
    s, d = x.shape
    dff = wg.shape[1]
    row = pl.BlockSpec((tm, d), lambda i: (i, 0))
    return pl.pallas_call(
        functools.partial(_ffn_kernel, final_norm=final_norm, dff_chunk=dff_chunk),
        grid=(s // tm,),
        in_specs=[row, _resident((1, d)), _resident((d, dff)), _resident((d, dff)),
                  _resident((dff, d)), _resident((1, d))],
        out_specs=row,
        out_shape=jax.ShapeDtypeStruct((s, d), F32),
        compiler_params=_cparams(1),
        name="ffn_final" if final_norm else "ffn",
    )(x, g, wg, wu, wd, gfin)


def _head_norm_rope(x, gain, head_mean, cos, sin, first_half):
    ms = _dot((x * x).astype(BF16), head_mean)
    halves = []
    for c in range(x.shape[1] // LANES):
        sl = slice(c * LANES, (c + 1) * LANES)
        y = x[:, sl] * lax.rsqrt(ms[:, sl] + RMS_EPS) * gain
        partner = jnp.where(first_half,
                            pltpu.roll(y, LANES - ROPE_HALF, 1),
                            pltpu.roll(y, ROPE_HALF, 1))
        halves.append(y * cos + partner * sin)
    return halves


def _head_slots(xc, extra, low):
    return jnp.where(low, xc, extra), jnp.where(low, pltpu.roll(xc, HEAD_DIM, 1), extra)


def _token_table(row_ref, col_ref):
    col = col_ref[...]
    return jnp.concatenate([jnp.broadcast_to(row_ref[r:r + 1, :], col.shape) + col
                            for r in range(row_ref.shape[0])], axis=0)


def _inproj_kernel(h_ref, g_ref, win_ref, wv_ref, qg_ref, kg_ref, cosr_ref, sinr_ref, cosc_ref,
                   sinc_ref, hm_ref, qx_ref, kx_ref, vone_ref, q_ref, kt_ref, v_ref, f_ref, sf_ref,
                   sa_ref, *, widths):
    qw, kw, fw, gw = widths
    n = _rms(h_ref[...], g_ref[...]).astype(BF16)
    cos = _token_table(cosr_ref, cosc_ref)
    sin = _token_table(sinr_ref, sinc_ref)
    head_mean = hm_ref[...]
    lane = lax.broadcasted_iota(jnp.int32, cos.shape, 1)
    first_half = (lane % ROPE_AXIS_DIM) < ROPE_HALF
    low = lane < HEAD_DIM

    def proj(lo, width):
        return _dot(n, win_ref[:, lo:lo + width])

    q = proj(0, qw)
    pair = head_mean.shape[0]
    slot_lo = 0
    for c in range(qw // pair):
        for qc in _head_norm_rope(q[:, c * pair:(c + 1) * pair], qg_ref[...], head_mean, cos, sin,
                                  first_half):
            for slot in _head_slots(qc * (LOG2E / math.sqrt(HEAD_DIM)), qx_ref[...], low):
                q_ref[:, slot_lo:slot_lo + LANES] = slot.astype(BF16)
                slot_lo += LANES
    k = proj(qw, kw)
    k_slots = []
    for c in range(kw // pair):
        for kc in _head_norm_rope(k[:, c * pair:(c + 1) * pair], kg_ref[...], head_mean, cos, sin,
                                  first_half):
            k_slots.extend(_head_slots(kc, kx_ref[...], low))
    kt_ref[...] = jnp.concatenate(k_slots, axis=1).T.astype(BF16)
    v_ref[...] = (_dot(n, wv_ref[...]) + vone_ref[...]).astype(BF16)
    lo = qw + 2 * kw
    f_ref[...] = proj(lo, fw).astype(BF16)
    lo += fw
    sf_ref[...] = jax.nn.sigmoid(proj(lo, gw)).astype(sf_ref.dtype)
    lo += gw
    sa_ref[...] = jax.nn.sigmoid(proj(lo, gw)).astype(sa_ref.dtype)


def _inproj(h, g, win, wv, qg, kg, rope, head_mean, qx, kx, vone, *, widths, tm):
    s, d = h.shape
    qw, kw, fw, gw = widths
    vw = wv.shape[1]
    slots = LANES // HEAD_DIM
    assert tm % GRID_W == 0

    def rows(w):
        return pl.BlockSpec((tm, w), lambda i: (i, 0))

    lane_row = _resident((1, LANES))
    grid_rows = pl.BlockSpec((tm // GRID_W, LANES), lambda i: (i, 0))
    grid_cols = _resident((GRID_W, LANES))
    return pl.pallas_call(
        functools.partial(_inproj_kernel, widths=widths),
        grid=(s // tm,),
        in_specs=[rows(d), _resident((1, d)), _resident(win.shape), _resident(wv.shape), lane_row,
                  lane_row, grid_rows, grid_rows, grid_cols, grid_cols,
                  _resident(head_mean.shape), lane_row, lane_row, _resident((1, vw))],
        out_specs=[rows(qw * slots), pl.BlockSpec((kw * slots, tm), lambda i: (0, i)), rows(vw),
                   rows(fw), rows(gw), rows(gw)],
        out_shape=[jax.ShapeDtypeStruct((s, qw * slots), BF16),
                   jax.ShapeDtypeStruct((kw * slots, s), BF16),
                   jax.ShapeDtypeStruct((s, vw), BF16), jax.ShapeDtypeStruct((s, fw), BF16),
                   jax.ShapeDtypeStruct((s, gw), BF16), jax.ShapeDtypeStruct((s, gw), BF16)],
        compiler_params=_cparams(1),
        name="inproj",
    )(h, g, win, wv, qg, kg, *rope, head_mean, qx, kx, vone)


def _fft1_kernel(x_ref, cs1_ref, a_ref):
    n1, nb2, width = x_ref.shape
    x = x_ref[...]
    tiles = [x[:, :, t * LANES:(t + 1) * LANES].reshape(n1, nb2 * LANES) for t in range(width // LANES)]
    for j in range(nb2):
        xj = jnp.concatenate([t[:, j * LANES:(j + 1) * LANES] for t in tiles], axis=1)
        a_ref[:, j * width:(j + 1) * width] = _dot(cs1_ref[...], xj).astype(a_ref.dtype)


def _fft1(x3, cs1, *, nb2):
    n1, n2, width = x3.shape
    return pl.pallas_call(
        _fft1_kernel,
        grid=(n2 // nb2,),
        in_specs=[pl.BlockSpec((n1, nb2, width), lambda j: (0, j, 0)), _resident((2 * n1, n1))],
        out_specs=pl.BlockSpec((2 * n1, nb2 * width), lambda j: (0, j)),
        out_shape=jax.ShapeDtypeStruct((2 * n1, n2 * width), BF16),
        compiler_params=_cparams(1),
        name="fft1",
    )(x3, cs1)


def _fft2_kernel(ac_ref, as_ref, tc_ref, ts_ref, rot2_ref, mix_ref, o_ref):
    kb, n2, width = ac_ref.shape
    gd = FOURIER_GROUP_DIM
    per_k1 = []
    for kk in range(kb):
        ac = ac_ref[kk].astype(F32)
        as_ = as_ref[kk].astype(F32)
        tc = jnp.broadcast_to(tc_ref[0][:, kk:kk + 1], (n2, width))
        ts = jnp.broadcast_to(ts_ref[0][:, kk:kk + 1], (n2, width))
        b = jnp.concatenate([ac * tc - as_ * ts, as_ * tc + ac * ts], axis=0).astype(BF16)
        p = _dot(rot2_ref[...], b).astype(BF16)
        stacked = jnp.concatenate(
            [jnp.concatenate([p[:n2, g * gd:(g + 1) * gd], p[n2:, g * gd:(g + 1) * gd]], axis=1)
             for g in range(width // gd)], axis=0)
        per_k1.append(_dot(stacked, mix_ref[...]))
    for g in range(width // gd):
        side_by_side = jnp.concatenate([m[g * n2:(g + 1) * n2] for m in per_k1], axis=1)
        o_ref[:, :, g * gd:(g + 1) * gd] = side_by_side.reshape(n2, kb, gd).astype(o_ref.dtype)


def _fft2(a3, tc, ts, cs2, mix, *, kb):
    two_n1, n2, width = a3.shape
    n1 = two_n1 // 2
    nblk = n1 // kb
    return pl.pallas_call(
        _fft2_kernel,
        grid=(nblk,),
        in_specs=[pl.BlockSpec((kb, n2, width), lambda j: (j, 0, 0)),
                  pl.BlockSpec((kb, n2, width), lambda j: (j + nblk, 0, 0)),
                  pl.BlockSpec((1, n2, kb), lambda j: (j, 0, 0)),
                  pl.BlockSpec((1, n2, kb), lambda j: (j, 0, 0)),
                  _resident(cs2.shape), _resident(mix.shape)],
        out_specs=pl.BlockSpec((n2, kb, width), lambda j: (0, j, 0)),
        out_shape=jax.ShapeDtypeStruct((n2, n1, width), F32),
        compiler_params=_cparams(1),
        name="fft2",
    )(a3, a3, tc, ts, cs2, mix)


def _dft_tables(s):
    n1 = s // FFT_INNER

    def cos_sin(n, scale):
        ang = 2.0 * np.pi * (np.outer(np.arange(n), np.arange(n)) % n) / n
        return np.cos(ang) * scale, np.sin(ang) * scale

    c1, s1 = cos_sin(n1, n1 ** -0.5)
    c2, s2 = cos_sin(FFT_INNER, FFT_INNER ** -0.5)
    cc, sc = cos_sin(FOURIER_GROUP_DIM, FOURIER_GROUP_DIM ** -0.5)
    as_bf16 = lambda a: jnp.asarray(a, dtype=F32).astype(BF16)
    return (as_bf16(np.concatenate([c1, s1], axis=0)), as_bf16(np.block([[c2, -s2], [s2, c2]])),
            as_bf16(np.concatenate([cc, -sc], axis=0)))


def _twiddles(s, kb):
    n1 = s // FFT_INNER
    ang = 2.0 * np.pi * (np.outer(np.arange(n1), np.arange(FFT_INNER)) % s) / s
    ang = ang.reshape(n1 // kb, kb, FFT_INNER).transpose(0, 2, 1)
    return jnp.asarray(np.cos(ang), dtype=F32), jnp.asarray(np.sin(ang), dtype=F32)


def _attn_kernel(q_ref, kt_ref, v_ref, o_ref, *, tk, online):
    tq = q_ref.shape[0]
    qs = jnp.concatenate([q_ref[:, g * LANES:(g + 1) * LANES] for g in range(GQA_GROUP)], axis=0)
    rows = GQA_GROUP * tq
    n_chunks = kt_ref.shape[1] // tk
    acc0 = jnp.zeros((rows, LANES), F32)

    def scores(j):
        off = pl.multiple_of(j * tk, tk)
        return _dot(qs, kt_ref[:, pl.ds(off, tk)]), v_ref[pl.ds(off, tk), :]

    if online:
        def body(j, carry):
            m, acc = carry
            s, v = scores(j)
            m_new = jnp.maximum(m, jnp.max(s, axis=1, keepdims=True))
            p = jnp.exp2(s - m_new).astype(BF16)
            return m_new, acc * jnp.exp2(m - m_new) + _dot(p, v)

        m0 = jnp.full((rows, 1), -jnp.inf, F32)
        _, acc = lax.fori_loop(0, n_chunks, body, (m0, acc0))
    else:
        def body(j, acc):
            s, v = scores(j)
            return acc + _dot(jnp.exp2(s).astype(BF16), v)

        acc = lax.fori_loop(0, n_chunks, body, acc0)
    out = acc[:, :HEAD_DIM] / acc[:, HEAD_DIM:HEAD_DIM + 1]
    o_ref[...] = jnp.concatenate([out[g * tq:(g + 1) * tq] for g in range(GQA_GROUP)],
                                 axis=1).astype(BF16)


def _attn(q, kt, v, *, tq, tk, online):
    s = q.shape[0]
    gw = GQA_GROUP * HEAD_DIM
    return pl.pallas_call(
        functools.partial(_attn_kernel, tk=tk, online=online),
        grid=(N_KV_HEADS, s // tq),
        in_specs=[pl.BlockSpec((tq, GQA_GROUP * LANES), lambda h, i: (i, h)),
                  pl.BlockSpec((LANES, s), lambda h, i: (h, 0)),
                  pl.BlockSpec((s, LANES), lambda h, i: (0, h))],
        out_specs=pl.BlockSpec((tq, gw), lambda h, i: (i, h)),
        out_shape=jax.ShapeDtypeStruct((s, N_KV_HEADS * gw), BF16),
        compiler_params=_cparams(2),
        name="attn_online" if online else "attn",
    )(q, kt, v)


def _memkv_kernel(mem_ref, g_ref, wk_ref, wv_ref, ktm_ref, vm_ref):
    mn = _rms(mem_ref[...], g_ref[...]).astype(BF16)
    kt = _dot(mn, wk_ref[...]).T
    v = _dot(mn, wv_ref[...])
    feat_row = lax.broadcasted_iota(jnp.int32, kt.shape, 0) // HEAD_DIM
    feat_col = lax.broadcasted_iota(jnp.int32, v.shape, 1) // HEAD_DIM
    for h in range(N_CROSS_HEADS):
        ktm_ref[h] = jnp.where(feat_row == h, kt, 0.0).astype(BF16)
        vm_ref[h] = jnp.where(feat_col == h, v, 0.0).astype(BF16)


def _memkv(mem, g, wk, wv):
    m, d = mem.shape
    cw = wk.shape[1]
    whole = lambda shape: pl.BlockSpec(shape, lambda: (0,) * len(shape))
    return pl.pallas_call(
        _memkv_kernel,
        in_specs=[whole((m, d)), whole((1, d)), whole((d, cw)), whole((d, cw))],
        out_specs=[whole((N_CROSS_HEADS, cw, m)), whole((N_CROSS_HEADS, m, cw))],
        out_shape=[jax.ShapeDtypeStruct((N_CROSS_HEADS, cw, m), BF16),
                   jax.ShapeDtypeStruct((N_CROSS_HEADS, m, cw), BF16)],
        compiler_params=pltpu.CompilerParams(vmem_limit_bytes=VMEM_LIMIT),
        name="memkv",
    )(mem, g, wk, wv)


def _merge_kernel(h_ref, fm_ref, o_ref, sf_ref, sa_ref, wfb_ref, wab_ref, wout_ref, gc_ref,
                  wcq_ref, ktm_ref, vm_ref, wco_ref, out_ref, *, sub_tiles):
    sub = h_ref.shape[0] // sub_tiles
    tiles = [slice(r * sub, (r + 1) * sub) for r in range(sub_tiles)]
    y_f = [_dot(fm_ref[t, :].astype(BF16), wfb_ref[...]) for t in tiles]
    y_a = [_dot(o_ref[t, :], wab_ref[...]) for t in tiles]
    merged = [(sf_ref[t, :] * f + sa_ref[t, :] * a).astype(BF16) for t, f, a in zip(tiles, y_f, y_a)]
    h = [h_ref[t, :] + _dot(m, wout_ref[...]) for t, m in zip(tiles, merged)]
    n = [_rms(x, gc_ref[...]).astype(BF16) for x in h]
    qc = [(_dot(x, wcq_ref[...]) * (1.0 / math.sqrt(HEAD_DIM))).astype(BF16) for x in n]
    oc = [None] * sub_tiles
    for hh in range(N_CROSS_HEADS):
        s = [_dot(x, ktm_ref[hh]) for x in qc]
        e = [jnp.exp(x - jnp.max(x, axis=1, keepdims=True)) for x in s]
        p = [(x / jnp.sum(x, axis=1, keepdims=True)).astype(BF16) for x in e]
        head = [_dot(x, vm_ref[hh]) for x in p]
        oc = [x if acc is None else acc + x for acc, x in zip(oc, head)]
    for t, x, o in zip(tiles, h, oc):
        out_ref[t, :] = x + _dot(o.astype(BF16), wco_ref[...])


def _merge(h, fm, o, sf, sa, wfb, wab, wout, gc, wcq, ktm, vm, wco, *, tm, sub_tiles):
    s, d = h.shape

    def rows(w):
        return pl.BlockSpec((tm, w), lambda i: (i, 0))

    weights = [wfb, wab, wout, gc, wcq, ktm, vm, wco]
    return pl.pallas_call(
        functools.partial(_merge_kernel, sub_tiles=sub_tiles),
        grid=(s // tm,),
        in_specs=[rows(d), rows(fm.shape[1]), rows(o.shape[1]), rows(d), rows(d)]
                 + [_resident(w.shape) for w in weights],
        out_specs=rows(d),
        out_shape=jax.ShapeDtypeStruct((s, d), F32),
        compiler_params=_cparams(1),
        name="merge",
    )(h, fm, o, sf, sa, *weights)


def _rope_tables(s):
    f32 = np.float32
    lane = np.arange(LANES)
    inv = f32(1.0) / (f32(ROPE_THETA) ** (np.arange(0, ROPE_AXIS_DIM, 2, dtype=f32) / f32(ROPE_AXIS_DIM)))
    inv_lane = inv[lane % ROPE_HALF]
    row_lane = (lane % HEAD_DIM) < ROPE_AXIS_DIM
    sign = np.where((lane % ROPE_AXIS_DIM) < ROPE_HALF, f32(-1.0), f32(1.0))
    ang_r = np.arange(s // GRID_W, dtype=f32)[:, None] * inv_lane[None, :]
    ang_c = np.arange(GRID_W, dtype=f32)[:, None] * inv_lane[None, :]
    tables = (np.where(row_lane, np.cos(ang_r), 0.0), np.where(row_lane, np.sin(ang_r) * sign, 0.0),
              np.where(row_lane, 0.0, np.cos(ang_c)), np.where(row_lane, 0.0, np.sin(ang_c) * sign))
    return tuple(jnp.asarray(t, dtype=F32) for t in tables)


def _pick_tile(s, want):
    t = min(want, s)
    while s % t:
        t //= 2
    return t


def _tiles(s):
    tm = _pick_tile(s, 512)
    return dict(
        tokens=tm,
        ffn_tokens=_pick_tile(s, 2 * tm),
        ffn_hidden=3 * MXU_WIDTH,
        merge_tokens=_pick_tile(s, 2 * tm),
        merge_sub_tiles=4,
        fft_n2=16,
        fft_k1=8,
        attn_q=_pick_tile(s, 512),
        attn_kv=_pick_tile(s, 4096),
        attn_online_q=_pick_tile(s, 128),
        attn_online_kv=_pick_tile(s, 512),
    )


def kernel(x, mem, g_ffn1, w1_gate, w1_up, w1_down, g_mix, w_in, q_gain, k_gain, w_fourier_branch, w_attn_branch, w_out, g_cross, g_mem, w_cq, w_ck, w_cv, w_co, g_ffn2, w2_gate, w2_up, w2_down, g_final):
    batch, s, d = x.shape
    depth = w_in.shape[0]
    qw = w_attn_branch.shape[1]
    kw = N_KV_HEADS * HEAD_DIM
    fw = w_fourier_branch.shape[1]
    vw = N_KV_HEADS * LANES
    assert s % (FFT_INNER * 8) == 0 and s % GRID_W == 0
    assert qw == N_KV_HEADS * GQA_GROUP * HEAD_DIM and fw == FOURIER_GROUPS * FOURIER_GROUP_DIM
    n1 = s // FFT_INNER
    tiles = _tiles(s)
    row = lambda g: g.reshape(1, -1).astype(F32)
    bf = lambda w: w.astype(BF16)

    rope = _rope_tables(s)
    cs1, cs2, mix = _dft_tables(s)
    tw_c, tw_s = _twiddles(s, tiles["fft_k1"])
    lane = np.arange(LANES)
    mxu_lane = np.arange(MXU_WIDTH)
    head_mean = jnp.asarray((mxu_lane[:, None] // HEAD_DIM == mxu_lane[None, :] // HEAD_DIM) / HEAD_DIM,
                            dtype=F32).astype(BF16)
    vone = jnp.asarray((np.arange(vw) % LANES == HEAD_DIM).astype(np.float32)).reshape(1, vw)
    v_col = np.arange(kw)
    spread = np.zeros((kw, vw), np.float32)
    spread[v_col, v_col // HEAD_DIM * LANES + v_col % HEAD_DIM] = 1.0
    spread = jnp.asarray(spread)

    outs = []
    for b in range(batch):
        h = x[b]
        for l in range(depth):
            h = _ffn(h, row(g_ffn1[l]), bf(w1_gate[l]), bf(w1_up[l]), bf(w1_down[l]),
                     row(g_final), final_norm=False, tm=tiles["ffn_tokens"], dff_chunk=tiles["ffn_hidden"])

            wv = bf(jnp.dot(w_in[l][:, qw + kw:qw + 2 * kw], spread))
            shift = (LOG2E * math.sqrt(HEAD_DIM)) * jnp.max(jnp.abs(q_gain[l])) * jnp.max(jnp.abs(k_gain[l]))
            extra_lane = jnp.asarray((lane == HEAD_DIM).astype(np.float32)).reshape(1, LANES)
            q, kt, v, f, sf, sa = _inproj(
                h, row(g_mix[l]), bf(w_in[l]), wv, row(jnp.tile(q_gain[l], LANES // HEAD_DIM)),
                row(jnp.tile(k_gain[l], LANES // HEAD_DIM)), rope, head_mean,
                extra_lane, -shift * extra_lane, vone, widths=(qw, kw, fw, d), tm=tiles["tokens"])

            a = _fft1(f.reshape(n1, FFT_INNER, fw), cs1, nb2=tiles["fft_n2"])
            fm = _fft2(a.reshape(2 * n1, FFT_INNER, fw), tw_c, tw_s, cs2, mix, kb=tiles["fft_k1"])
            fm = fm.reshape(s, fw)

            o = lax.cond(
                shift <= MAX_SAFE_SHIFT,
                functools.partial(_attn, tq=tiles["attn_q"], tk=tiles["attn_kv"], online=False),
                functools.partial(_attn, tq=tiles["attn_online_q"], tk=tiles["attn_online_kv"],
                                  online=True),
                q, kt, v)

            ktm, vm = _memkv(mem[b], row(g_mem[l]), bf(w_ck[l]), bf(w_cv[l]))
            h = _merge(h, fm, o, sf, sa, bf(w_fourier_branch[l]), bf(w_attn_branch[l]), bf(w_out[l]),
                       row(g_cross[l]), bf(w_cq[l]), ktm, vm, bf(w_co[l]),
                       tm=tiles["merge_tokens"], sub_tiles=tiles["merge_sub_tiles"])

            h = _ffn(h, row(g_ffn2[l]), bf(w2_gate[l]), bf(w2_up[l]), bf(w2_down[l]),
                     row(g_final), final_norm=(l == depth - 1), tm=tiles["ffn_tokens"],
                     dff_chunk=tiles["ffn_hidden"])
        outs.append(h)
    return jnp.stack(outs, axis=0)
```

```python
import functools
import math

import numpy as np
import jax
import jax.numpy as jnp
from jax import lax
from jax.experimental import pallas as pl
from jax.experimental.pallas import tpu as pltpu

F32 = jnp.float32
BF16 = jnp.bfloat16

LANES = 128
MXU_WIDTH = 256
GRID_W = 64
HEAD_DIM = 64
N_KV_HEADS = 4
GQA_GROUP = 4
N_CROSS_HEADS = 4
FOURIER_GROUPS = 4
FOURIER_GROUP_DIM = 128
ROPE_AXIS_DIM = HEAD_DIM // 2
ROPE_HALF = ROPE_AXIS_DIM // 2
ROPE_THETA = 10000.0
RMS_EPS = 1e-6
FFT_INNER = 128
VMEM_LIMIT = 56 * 1024 * 1024
LOG2E = math.log2(math.e)
MAX_SAFE_SHIFT = 40.0


def _cparams(n_axes):
    return pltpu.CompilerParams(dimension_semantics=("arbitrary",) * n_axes,
                                vmem_limit_bytes=VMEM_LIMIT)


def _resident(shape):
    zeros = (0,) * len(shape)
    return pl.BlockSpec(shape, lambda *_: zeros, pipeline_mode=pl.Buffered(1))


def _dot(a, b):
    return jnp.dot(a, b, preferred_element_type=F32)


def _rms(x, g):
    y = x * lax.rsqrt(jnp.mean(x * x, axis=-1, keepdims=True) + RMS_EPS)
    return y * g


def _ffn_kernel(x_ref, g_ref, wg_ref, wu_ref, wd_ref, gfin_ref, o_ref, *, final_norm, dff_chunk):
    x = x_ref[...]
    xn = _rms(x, g_ref[...]).astype(BF16)
    dff = wg_ref.shape[1]
    acc = None
    for lo in range(0, dff, dff_chunk):
        hi = min(lo + dff_chunk, dff)
        gate = _dot(xn, wg_ref[:, lo:hi])
        up = _dot(xn, wu_ref[:, lo:hi])
        act = (gate * jax.nn.sigmoid(gate) * up).astype(BF16)
        part = _dot(act, wd_ref[lo:hi, :])
        acc = part if acc is None else acc + part
    h = x + 0.5 * acc
    if final_norm:
        h = _rms(h, gfin_ref[...])
    o_ref[...] = h


def _ffn(x, g, wg, wu, wd, gfin, *, final_norm, tm, dff_chunk):
    s, d = x.shape
    dff = wg.shape[1]
    row = pl.BlockSpec((tm, d), lambda i: (i, 0))
    return pl.pallas_call(
        functools.partial(_ffn_kernel, final_norm=final_norm, dff_chunk=dff_chunk),
        grid=(s // tm,),
        in_specs=[row, _resident((1, d)), _resident((d, dff)), _resident((d, dff)),
                  _resident((dff, d)), _resident((1, d))],
        out_specs=row,
        out_shape=jax.ShapeDtypeStruct((s, d), F32),
        compiler_params=_cparams(1),
        name="ffn_final" if final_norm else "ffn",
    )(x, g, wg, wu, wd, gfin)


def _head_norm_rope(x, gain, head_mean, cos, sin, first_half):
    ms = _dot((x * x).astype(BF16), head_mean)
    halves = []
    for c in range(x.shape[1] // LANES):
        sl = slice(c * LANES, (c + 1) * LANES)
        y = x[:, sl] * lax.rsqrt(ms[:, sl] + RMS_EPS) * gain
        partner = jnp.where(first_half,
                            pltpu.roll(y, LANES - ROPE_HALF, 1),
                            pltpu.roll(y, ROPE_HALF, 1))
        halves.append(y * cos + partner * sin)
    return halves


def _head_slots(xc, extra, low):
    return jnp.where(low, xc, extra), jnp.where(low, pltpu.roll(xc, HEAD_DIM, 1), extra)


def _token_table(row_ref, col_ref):
    col = col_ref[...]
    return jnp.concatenate([jnp.broadcast_to(row_ref[r:r + 1, :], col.shape) + col
                            for r in range(row_ref.shape[0])], axis=0)


def _inproj_kernel(h_ref, g_ref, win_ref, wv_ref, qg_ref, kg_ref, cosr_ref, sinr_ref, cosc_ref,
                   sinc_ref, hm_ref, qx_ref, kx_ref, vone_ref, q_ref, kt_ref, v_ref, f_ref, sf_ref,
                   sa_ref, *, widths):
    qw, kw, fw, gw = widths
    n = _rms(h_ref[...], g_ref[...]).astype(BF16)
    cos = _token_table(cosr_ref, cosc_ref)
    sin = _token_table(sinr_ref, sinc_ref)
    head_mean = hm_ref[...]
    lane = lax.broadcasted_iota(jnp.int32, cos.shape, 1)
    first_half = (lane % ROPE_AXIS_DIM) < ROPE_HALF
    low = lane < HEAD_DIM

    def proj(lo, width):
        return _dot(n, win_ref[:, lo:lo + width])

    q = proj(0, qw)
    pair = head_mean.shape[0]
    slot_lo = 0
    for c in range(qw // pair):
        for qc in _head_norm_rope(q[:, c * pair:(c + 1) * pair], qg_ref[...], head_mean, cos, sin,
                                  first_half):
            for slot in _head_slots(qc * (LOG2E / math.sqrt(HEAD_DIM)), qx_ref[...], low):
                q_ref[:, slot_lo:slot_lo + LANES] = slot.astype(BF16)
                slot_lo += LANES
    k = proj(qw, kw)
    k_slots = []
    for c in range(kw // pair):
        for kc in _head_norm_rope(k[:, c * pair:(c + 1) * pair], kg_ref[...], head_mean, cos, sin,
                                  first_half):
            k_slots.extend(_head_slots(kc, kx_ref[...], low))
    kt_ref[...] = jnp.concatenate(k_slots, axis=1).T.astype(BF16)
    v_ref[...] = (_dot(n, wv_ref[...]) + vone_ref[...]).astype(BF16)
    lo = qw + 2 * kw
    f_ref[...] = proj(lo, fw).astype(BF16)
    lo += fw
    sf_ref[...] = jax.nn.sigmoid(proj(lo, gw)).astype(sf_ref.dtype)
    lo += gw
    sa_ref[...] = jax.nn.sigmoid(proj(lo, gw)).astype(sa_ref.dtype)


def _inproj(h, g, win, wv, qg, kg, rope, head_mean, qx, kx, vone, *, widths, tm):
    s, d = h.shape
    qw, kw, fw, gw = widths
    vw = wv.shape[1]
    slots = LANES // HEAD_DIM
    assert tm % GRID_W == 0

    def rows(w):
        return pl.BlockSpec((tm, w), lambda i: (i, 0))

    lane_row = _resident((1, LANES))
    grid_rows = pl.BlockSpec((tm // GRID_W, LANES), lambda i: (i, 0))
    grid_cols = _resident((GRID_W, LANES))
    return pl.pallas_call(
        functools.partial(_inproj_kernel, widths=widths),
        grid=(s // tm,),
        in_specs=[rows(d), _resident((1, d)), _resident(win.shape), _resident(wv.shape), lane_row,
                  lane_row, grid_rows, grid_rows, grid_cols, grid_cols,
                  _resident(head_mean.shape), lane_row, lane_row, _resident((1, vw))],
        out_specs=[rows(qw * slots), pl.BlockSpec((kw * slots, tm), lambda i: (0, i)), rows(vw),
                   rows(fw), rows(gw), rows(gw)],
        out_shape=[jax.ShapeDtypeStruct((s, qw * slots), BF16),
                   jax.ShapeDtypeStruct((kw * slots, s), BF16),
                   jax.ShapeDtypeStruct((s, vw), BF16), jax.ShapeDtypeStruct((s, fw), BF16),
                   jax.ShapeDtypeStruct((s, gw), BF16), jax.ShapeDtypeStruct((s, gw), BF16)],
        compiler_params=_cparams(1),
        name="inproj",
    )(h, g, win, wv, qg, kg, *rope, head_mean, qx, kx, vone)


def _fft1_kernel(x_ref, cs1_ref, a_ref):
    n1, nb2, width = x_ref.shape
    x = x_ref[...]
    tiles = [x[:, :, t * LANES:(t + 1) * LANES].reshape(n1, nb2 * LANES) for t in range(width // LANES)]
    for j in range(nb2):
        xj = jnp.concatenate([t[:, j * LANES:(j + 1) * LANES] for t in tiles], axis=1)
        a_ref[:, j * width:(j + 1) * width] = _dot(cs1_ref[...], xj).astype(a_ref.dtype)


def _fft1(x3, cs1, *, nb2):
    n1, n2, width = x3.shape
    return pl.pallas_call(
        _fft1_kernel,
        grid=(n2 // nb2,),
        in_specs=[pl.BlockSpec((n1, nb2, width), lambda j: (0, j, 0)), _resident((2 * n1, n1))],
        out_specs=pl.BlockSpec((2 * n1, nb2 * width), lambda j: (0, j)),
        out_shape=jax.ShapeDtypeStruct((2 * n1, n2 * width), BF16),
        compiler_params=_cparams(1),
        name="fft1",
    )(x3, cs1)


def _fft2_kernel(ac_ref, as_ref, tc_ref, ts_ref, rot2_ref, mix_ref, o_ref):
    kb, n2, width = ac_ref.shape
    gd = FOURIER_GROUP_DIM
    per_k1 = []
    for kk in range(kb):
        ac = ac_ref[kk].astype(F32)
        as_ = as_ref[kk].astype(F32)
        tc = jnp.broadcast_to(tc_ref[0][:, kk:kk + 1], (n2, width))
        ts = jnp.broadcast_to(ts_ref[0][:, kk:kk + 1], (n2, width))
        b = jnp.concatenate([ac * tc - as_ * ts, as_ * tc + ac * ts], axis=0).astype(BF16)
        p = _dot(rot2_ref[...], b).astype(BF16)
        stacked = jnp.concatenate(
            [jnp.concatenate([p[:n2, g * gd:(g + 1) * gd], p[n2:, g * gd:(g + 1) * gd]], axis=1)
             for g in range(width // gd)], axis=0)
        per_k1.append(_dot(stacked, mix_ref[...]))
    for g in range(width // gd):
        side_by_side = jnp.concatenate([m[g * n2:(g + 1) * n2] for m in per_k1], axis=1)
        o_ref[:, :, g * gd:(g + 1) * gd] = side_by_side.reshape(n2, kb, gd).astype(o_ref.dtype)


def _fft2(a3, tc, ts, cs2, mix, *, kb):
    two_n1, n2, width = a3.shape
    n1 = two_n1 // 2
    nblk = n1 // kb
    return pl.pallas_call(
        _fft2_kernel,
        grid=(nblk,),
        in_specs=[pl.BlockSpec((kb, n2, width), lambda j: (j, 0, 0)),
                  pl.BlockSpec((kb, n2, width), lambda j: (j + nblk, 0, 0)),
                  pl.BlockSpec((1, n2, kb), lambda j: (j, 0, 0)),
                  pl.BlockSpec((1, n2, kb), lambda j: (j, 0, 0)),
                  _resident(cs2.shape), _resident(mix.shape)],
        out_specs=pl.BlockSpec((n2, kb, width), lambda j: (0, j, 0)),
        out_shape=jax.ShapeDtypeStruct((n2, n1, width), BF16),
        compiler_params=_cparams(1),
        name="fft2",
    )(a3, a3, tc, ts, cs2, mix)


def _dft_tables(s):
    n1 = s // FFT_INNER

    def cos_sin(n, scale):
        ang = 2.0 * np.pi * (np.outer(np.arange(n), np.arange(n)) % n) / n
        return np.cos(ang) * scale, np.sin(ang) * scale

    c1, s1 = cos_sin(n1, n1 ** -0.5)
    c2, s2 = cos_sin(FFT_INNER, FFT_INNER ** -0.5)
    cc, sc = cos_sin(FOURIER_GROUP_DIM, FOURIER_GROUP_DIM ** -0.5)
    as_bf16 = lambda a: jnp.asarray(a, dtype=F32).astype(BF16)
    return (as_bf16(np.concatenate([c1, s1], axis=0)), as_bf16(np.block([[c2, -s2], [s2, c2]])),
            as_bf16(np.concatenate([cc, -sc], axis=0)))


def _twiddles(s, kb):
    n1 = s // FFT_INNER
    ang = 2.0 * np.pi * (np.outer(np.arange(n1), np.arange(FFT_INNER)) % s) / s
    ang = ang.reshape(n1 // kb, kb, FFT_INNER).transpose(0, 2, 1)
    return jnp.asarray(np.cos(ang), dtype=F32), jnp.asarray(np.sin(ang), dtype=F32)


def _attn_kernel(q_ref, kt_ref, v_ref, *rest, tk, online, side_blocks):
    n_side = len(side_blocks)
    o_ref = rest[n_side]
    step = pl.program_id(0) * pl.num_programs(1) + pl.program_id(1)
    for src, dst, nb in zip(rest[:n_side], rest[n_side + 1:], side_blocks):
        @pl.when(step < nb)
        def _(src=src, dst=dst):
            dst[...] = src[...].astype(dst.dtype)
    tq = q_ref.shape[0]
    qs = jnp.concatenate([q_ref[:, g * LANES:(g + 1) * LANES] for g in range(GQA_GROUP)], axis=0)
    rows = GQA_GROUP * tq
    n_chunks = kt_ref.shape[1] // tk
    acc0 = jnp.zeros((rows, LANES), F32)

    def scores(j):
        off = pl.multiple_of(j * tk, tk)
        return _dot(qs, kt_ref[:, pl.ds(off, tk)]), v_ref[pl.ds(off, tk), :]

    if online:
        def body(j, carry):
            m, acc = carry
            s, v = scores(j)
            m_new = jnp.maximum(m, jnp.max(s, axis=1, keepdims=True))
            p = jnp.exp2(s - m_new).astype(BF16)
            return m_new, acc * jnp.exp2(m - m_new) + _dot(p, v)

        m0 = jnp.full((rows, 1), -jnp.inf, F32)
        _, acc = lax.fori_loop(0, n_chunks, body, (m0, acc0))
    else:
        def body(j, acc):
            s, v = scores(j)
            return acc + _dot(jnp.exp2(s).astype(BF16), v)

        acc = lax.fori_loop(0, n_chunks, body, acc0)
    out = acc[:, :HEAD_DIM] / acc[:, HEAD_DIM:HEAD_DIM + 1]
    o_ref[...] = jnp.concatenate([out[g * tq:(g + 1) * tq] for g in range(GQA_GROUP)],
                                 axis=1).astype(BF16)


def _row_blocks(rows, max_blocks):
    for nb in range(min(max_blocks, rows // 16), 0, -1):
        if rows % nb == 0 and (rows // nb) % 16 == 0:
            return nb
    return 1


def _attn(q, kt, v, side, *, tq, tk, online):
    s = q.shape[0]
    gw = GQA_GROUP * HEAD_DIM
    q_blocks = s // tq
    side_blocks = tuple(_row_blocks(w.shape[0], N_KV_HEADS * q_blocks) for w in side)
    side_specs = [pl.BlockSpec((w.shape[0] // nb, w.shape[1]),
                               lambda h, i, nb=nb: (jnp.minimum(h * q_blocks + i, nb - 1), 0))
                  for w, nb in zip(side, side_blocks)]
    outs = pl.pallas_call(
        functools.partial(_attn_kernel, tk=tk, online=online, side_blocks=side_blocks),
        grid=(N_KV_HEADS, q_blocks),
        in_specs=[pl.BlockSpec((tq, GQA_GROUP * LANES), lambda h, i: (i, h)),
                  pl.BlockSpec((LANES, s), lambda h, i: (h, 0)),
                  pl.BlockSpec((s, LANES), lambda h, i: (0, h))] + side_specs,
        out_specs=[pl.BlockSpec((tq, gw), lambda h, i: (i, h))] + side_specs,
        out_shape=[jax.ShapeDtypeStruct((s, N_KV_HEADS * gw), BF16)]
                  + [jax.ShapeDtypeStruct(w.shape, BF16) for w in side],
        compiler_params=_cparams(2),
        name="attn_online" if online else "attn",
    )(q, kt, v, *side)
    return outs[0], outs[1:]


def _memkv_kernel(mem_ref, g_ref, wk_ref, wv_ref, ktm_ref, vm_ref):
    mn = _rms(mem_ref[...], g_ref[...]).astype(BF16)
    kt = _dot(mn, wk_ref[...]).T
    v = _dot(mn, wv_ref[...])
    feat_row = lax.broadcasted_iota(jnp.int32, kt.shape, 0) // HEAD_DIM
    feat_col = lax.broadcasted_iota(jnp.int32, v.shape, 1) // HEAD_DIM
    for h in range(N_CROSS_HEADS):
        ktm_ref[h] = jnp.where(feat_row == h, kt, 0.0).astype(BF16)
        vm_ref[h] = jnp.where(feat_col == h, v, 0.0).astype(BF16)


def _memkv(mem, g, wk, wv):
    m, d = mem.shape
    cw = wk.shape[1]
    whole = lambda shape: pl.BlockSpec(shape, lambda: (0,) * len(shape))
    return pl.pallas_call(
        _memkv_kernel,
        in_specs=[whole((m, d)), whole((1, d)), whole((d, cw)), whole((d, cw))],
        out_specs=[whole((N_CROSS_HEADS, cw, m)), whole((N_CROSS_HEADS, m, cw))],
        out_shape=[jax.ShapeDtypeStruct((N_CROSS_HEADS, cw, m), BF16),
                   jax.ShapeDtypeStruct((N_CROSS_HEADS, m, cw), BF16)],
        compiler_params=pltpu.CompilerParams(vmem_limit_bytes=VMEM_LIMIT),
        name="memkv",
    )(mem, g, wk, wv)


def _merge_kernel(h_ref, fm_ref, o_ref, sf_ref, sa_ref, wfb_ref, wab_ref, wout_ref, gc_ref,
                  wcq_ref, ktm_ref, vm_ref, wco_ref, out_ref, *, sub_tiles):
    sub = h_ref.shape[0] // sub_tiles
    tiles = [slice(r * sub, (r + 1) * sub) for r in range(sub_tiles)]
    y_f = [_dot(fm_ref[t, :], wfb_ref[...]) for t in tiles]
    y_a = [_dot(o_ref[t, :], wab_ref[...]) for t in tiles]
    merged = [(sf_ref[t, :] * f + sa_ref[t, :] * a).astype(BF16) for t, f, a in zip(tiles, y_f, y_a)]
    h = [h_ref[t, :] + _dot(m, wout_ref[...]) for t, m in zip(tiles, merged)]
    n = [_rms(x, gc_ref[...]).astype(BF16) for x in h]
    qc = [(_dot(x, wcq_ref[...]) * (1.0 / math.sqrt(HEAD_DIM))).astype(BF16) for x in n]
    oc = [None] * sub_tiles
    for hh in range(N_CROSS_HEADS):
        s = [_dot(x, ktm_ref[hh]) for x in qc]
        e = [jnp.exp(x - jnp.max(x, axis=1, keepdims=True)) for x in s]
        p = [(x / jnp.sum(x, axis=1, keepdims=True)).astype(BF16) for x in e]
        head = [_dot(x, vm_ref[hh]) for x in p]
        oc = [x if acc is None else acc + x for acc, x in zip(oc, head)]
    for t, x, o in zip(tiles, h, oc):
        out_ref[t, :] = x + _dot(o.astype(BF16), wco_ref[...])


def _merge(h, fm, o, sf, sa, wfb, wab, wout, gc, wcq, ktm, vm, wco, *, tm, sub_tiles):
    s, d = h.shape

    def rows(w):
        return pl.BlockSpec((tm, w), lambda i: (i, 0))

    weights = [wfb, wab, wout, gc, wcq, ktm, vm, wco]
    return pl.pallas_call(
        functools.partial(_merge_kernel, sub_tiles=sub_tiles),
        grid=(s // tm,),
        in_specs=[rows(d), rows(fm.shape[1]), rows(o.shape[1]), rows(d), rows(d)]
                 + [_resident(w.shape) for w in weights],
        out_specs=rows(d),
        out_shape=jax.ShapeDtypeStruct((s, d), F32),
        compiler_params=_cparams(1),
        name="merge",
    )(h, fm, o, sf, sa, *weights)


def _rope_tables(s):
    f32 = np.float32
    lane = np.arange(LANES)
    inv = f32(1.0) / (f32(ROPE_THETA) ** (np.arange(0, ROPE_AXIS_DIM, 2, dtype=f32) / f32(ROPE_AXIS_DIM)))
    inv_lane = inv[lane % ROPE_HALF]
    row_lane = (lane % HEAD_DIM) < ROPE_AXIS_DIM
    sign = np.where((lane % ROPE_AXIS_DIM) < ROPE_HALF, f32(-1.0), f32(1.0))
    ang_r = np.arange(s // GRID_W, dtype=f32)[:, None] * inv_lane[None, :]
    ang_c = np.arange(GRID_W, dtype=f32)[:, None] * inv_lane[None, :]
    tables = (np.where(row_lane, np.cos(ang_r), 0.0), np.where(row_lane, np.sin(ang_r) * sign, 0.0),
              np.where(row_lane, 0.0, np.cos(ang_c)), np.where(row_lane, 0.0, np.sin(ang_c) * sign))
    return tuple(jnp.asarray(t, dtype=F32) for t in tables)


def _pick_tile(s, want):
    t = min(want, s)
    while s % t:
        t //= 2
    return t


def _tiles(s):
    tm = _pick_tile(s, 512)
    return dict(
        tokens=tm,
        ffn_tokens=_pick_tile(s, 2 * tm),
        ffn_hidden=3 * MXU_WIDTH,
        merge_tokens=_pick_tile(s, 2 * tm),
        merge_sub_tiles=4,
        fft_n2=16,
        fft_k1=min(16, s // FFT_INNER),
        attn_q=_pick_tile(s, 512),
        attn_kv=_pick_tile(s, 4096),
        attn_online_q=_pick_tile(s, 128),
        attn_online_kv=_pick_tile(s, 512),
    )


def kernel(x, mem, g_ffn1, w1_gate, w1_up, w1_down, g_mix, w_in, q_gain, k_gain, w_fourier_branch, w_attn_branch, w_out, g_cross, g_mem, w_cq, w_ck, w_cv, w_co, g_ffn2, w2_gate, w2_up, w2_down, g_final):
    batch, s, d = x.shape
    depth = w_in.shape[0]
    qw = w_attn_branch.shape[1]
    kw = N_KV_HEADS * HEAD_DIM
    fw = w_fourier_branch.shape[1]
    vw = N_KV_HEADS * LANES
    assert s % (FFT_INNER * 8) == 0 and s % GRID_W == 0
    assert qw == N_KV_HEADS * GQA_GROUP * HEAD_DIM and fw == FOURIER_GROUPS * FOURIER_GROUP_DIM
    n1 = s // FFT_INNER
    tiles = _tiles(s)
    row = lambda g: g.reshape(1, -1).astype(F32)
    bf = lambda w: w.astype(BF16)

    rope = _rope_tables(s)
    cs1, cs2, mix = _dft_tables(s)
    tw_c, tw_s = _twiddles(s, tiles["fft_k1"])
    lane = np.arange(LANES)
    mxu_lane = np.arange(MXU_WIDTH)
    head_mean = jnp.asarray((mxu_lane[:, None] // HEAD_DIM == mxu_lane[None, :] // HEAD_DIM) / HEAD_DIM,
                            dtype=F32).astype(BF16)
    vone = jnp.asarray((np.arange(vw) % LANES == HEAD_DIM).astype(np.float32)).reshape(1, vw)
    v_col = np.arange(kw)
    spread = np.zeros((kw, vw), np.float32)
    spread[v_col, v_col // HEAD_DIM * LANES + v_col % HEAD_DIM] = 1.0
    spread = jnp.asarray(spread)

    outs = []
    for b in range(batch):
        h = x[b]
        for l in range(depth):
            h = _ffn(h, row(g_ffn1[l]), bf(w1_gate[l]), bf(w1_up[l]), bf(w1_down[l]),
                     row(g_final), final_norm=False, tm=tiles["ffn_tokens"], dff_chunk=tiles["ffn_hidden"])

            wv = bf(jnp.dot(w_in[l][:, qw + kw:qw + 2 * kw], spread))
            shift = (LOG2E * math.sqrt(HEAD_DIM)) * jnp.max(jnp.abs(q_gain[l])) * jnp.max(jnp.abs(k_gain[l]))
            extra_lane = jnp.asarray((lane == HEAD_DIM).astype(np.float32)).reshape(1, LANES)
            q, kt, v, f, sf, sa = _inproj(
                h, row(g_mix[l]), bf(w_in[l]), wv, row(jnp.tile(q_gain[l], LANES // HEAD_DIM)),
                row(jnp.tile(k_gain[l], LANES // HEAD_DIM)), rope, head_mean,
                extra_lane, -shift * extra_lane, vone, widths=(qw, kw, fw, d), tm=tiles["tokens"])

            a = _fft1(f.reshape(n1, FFT_INNER, fw), cs1, nb2=tiles["fft_n2"])
            fm = _fft2(a.reshape(2 * n1, FFT_INNER, fw), tw_c, tw_s, cs2, mix, kb=tiles["fft_k1"])
            fm = fm.reshape(s, fw)

            later = (w_fourier_branch[l], w_attn_branch[l], w_out[l], w_cq[l], w_co[l], w_ck[l],
                     w_cv[l], w2_gate[l], w2_up[l], w2_down[l])
            o, (wfb, wab, wout, wcq, wco, wck, wcv, w2g, w2u, w2d) = lax.cond(
                shift <= MAX_SAFE_SHIFT,
                functools.partial(_attn, tq=tiles["attn_q"], tk=tiles["attn_kv"], online=False),
                functools.partial(_attn, tq=tiles["attn_online_q"], tk=tiles["attn_online_kv"],
                                  online=True),
                q, kt, v, later)

            ktm, vm = _memkv(mem[b], row(g_mem[l]), wck, wcv)
            h = _merge(h, fm, o, sf, sa, wfb, wab, wout, row(g_cross[l]), wcq, ktm, vm, wco,
                       tm=tiles["merge_tokens"], sub_tiles=tiles["merge_sub_tiles"])

            h = _ffn(h, row(g_ffn2[l]), w2g, w2u, w2d, row(g_final), final_norm=(l == depth - 1),
                     tm=tiles["ffn_tokens"], dff_chunk=tiles["ffn_hidden"])
        outs.append(h)
    return jnp.stack(outs, axis=0)
```

```python
import functools
import math

import numpy as np
import jax
import jax.numpy as jnp
from jax import lax
from jax.experimental import pallas as pl
from jax.experimental.pallas import tpu as pltpu

F32 = jnp.float32
BF16 = jnp.bfloat16

LANES = 128
MXU_WIDTH = 256
GRID_W = 64
HEAD_DIM = 64
N_KV_HEADS = 4
GQA_GROUP = 4
N_CROSS_HEADS = 4
FOURIER_GROUPS = 4
FOURIER_GROUP_DIM = 128
ROPE_AXIS_DIM = HEAD_DIM // 2
ROPE_HALF = ROPE_AXIS_DIM // 2
ROPE_THETA = 10000.0
RMS_EPS = 1e-6
FFT_INNER = 128
VMEM_LIMIT = 56 * 1024 * 1024
LOG2E = math.log2(math.e)
MAX_SAFE_SHIFT = 40.0


def _cparams(n_axes):
    return pltpu.CompilerParams(dimension_semantics=("arbitrary",) * n_axes,
                                vmem_limit_bytes=VMEM_LIMIT)


def _resident(shape):
    zeros = (0,) * len(shape)
    return pl.BlockSpec(shape, lambda *_: zeros, pipeline_mode=pl.Buffered(1))


def _dot(a, b):
    return jnp.dot(a, b, preferred_element_type=F32)


def _rms(x, g):
    y = x * lax.rsqrt(jnp.mean(x * x, axis=-1, keepdims=True) + RMS_EPS)
    return y * g


def _ffn_kernel(x_ref, g_ref, wg_ref, wu_ref, wd_ref, gfin_ref, o_ref, *, final_norm, dff_chunk):
    x = x_ref[...]
    xn = _rms(x, g_ref[...]).astype(BF16)
    dff = wg_ref.shape[1]
    acc = None
    for lo in range(0, dff, dff_chunk):
        hi = min(lo + dff_chunk, dff)
        gate = _dot(xn, wg_ref[:, lo:hi])
        up = _dot(xn, wu_ref[:, lo:hi])
        act = (gate * jax.nn.sigmoid(gate) * up).astype(BF16)
        part = _dot(act, wd_ref[lo:hi, :])
        acc = part if acc is None else acc + part
    h = x + 0.5 * acc
    if final_norm:
        h = _rms(h, gfin_ref[...])
    o_ref[...] = h


def _ffn(x, g, wg, wu, wd, gfin, *, final_norm, tm, dff_chunk):
    s, d = x.shape
    dff = wg.shape[1]
    row = pl.BlockSpec((tm, d), lambda i: (i, 0))
    return pl.pallas_call(
        functools.partial(_ffn_kernel, final_norm=final_norm, dff_chunk=dff_chunk),
        grid=(s // tm,),
        in_specs=[row, _resident((1, d)), _resident((d, dff)), _resident((d, dff)),
                  _resident((dff, d)), _resident((1, d))],
        out_specs=row,
        out_shape=jax.ShapeDtypeStruct((s, d), F32),
        compiler_params=_cparams(1),
        name="ffn_final" if final_norm else "ffn",
    )(x, g, wg, wu, wd, gfin)


def _head_norm_rope(x, gain, head_mean, cos, sin, first_half):
    ms = _dot((x * x).astype(BF16), head_mean)
    halves = []
    for c in range(x.shape[1] // LANES):
        sl = slice(c * LANES, (c + 1) * LANES)
        y = x[:, sl] * lax.rsqrt(ms[:, sl] + RMS_EPS) * gain
        partner = jnp.where(first_half,
                            pltpu.roll(y, LANES - ROPE_HALF, 1),
                            pltpu.roll(y, ROPE_HALF, 1))
        halves.append(y * cos + partner * sin)
    return halves


def _head_slots(xc, extra, low):
    return jnp.where(low, xc, extra), jnp.where(low, pltpu.roll(xc, HEAD_DIM, 1), extra)


def _token_table(row_ref, col_ref):
    col = col_ref[...]
    return jnp.concatenate([jnp.broadcast_to(row_ref[r:r + 1, :], col.shape) + col
                            for r in range(row_ref.shape[0])], axis=0)


def _inproj_kernel(h_ref, g_ref, win_ref, wv_ref, qg_ref, kg_ref, cosr_ref, sinr_ref, cosc_ref,
                   sinc_ref, hm_ref, qx_ref, kx_ref, vone_ref, q_ref, kt_ref, v_ref, f_ref, sf_ref,
                   sa_ref, *, widths):
    qw, kw, fw, gw = widths
    n = _rms(h_ref[...], g_ref[...]).astype(BF16)
    cos = _token_table(cosr_ref, cosc_ref)
    sin = _token_table(sinr_ref, sinc_ref)
    head_mean = hm_ref[...]
    lane = lax.broadcasted_iota(jnp.int32, cos.shape, 1)
    first_half = (lane % ROPE_AXIS_DIM) < ROPE_HALF
    low = lane < HEAD_DIM

    def proj(lo, width):
        return _dot(n, win_ref[:, lo:lo + width])

    q = proj(0, qw)
    pair = head_mean.shape[0]
    slot_lo = 0
    for c in range(qw // pair):
        for qc in _head_norm_rope(q[:, c * pair:(c + 1) * pair], qg_ref[...], head_mean, cos, sin,
                                  first_half):
            for slot in _head_slots(qc * (LOG2E / math.sqrt(HEAD_DIM)), qx_ref[...], low):
                q_ref[:, slot_lo:slot_lo + LANES] = slot.astype(BF16)
                slot_lo += LANES
    k = proj(qw, kw)
    k_slots = []
    for c in range(kw // pair):
        for kc in _head_norm_rope(k[:, c * pair:(c + 1) * pair], kg_ref[...], head_mean, cos, sin,
                                  first_half):
            k_slots.extend(_head_slots(kc, kx_ref[...], low))
    kt_ref[...] = jnp.concatenate(k_slots, axis=1).T.astype(BF16)
    v_ref[...] = (_dot(n, wv_ref[...]) + vone_ref[...]).astype(BF16)
    lo = qw + 2 * kw
    f_ref[...] = proj(lo, fw).astype(BF16)
    lo += fw
    sf_ref[...] = jax.nn.sigmoid(proj(lo, gw)).astype(sf_ref.dtype)
    lo += gw
    sa_ref[...] = jax.nn.sigmoid(proj(lo, gw)).astype(sa_ref.dtype)


def _inproj(h, g, win, wv, qg, kg, rope, head_mean, qx, kx, vone, *, widths, tm):
    s, d = h.shape
    qw, kw, fw, gw = widths
    vw = wv.shape[1]
    slots = LANES // HEAD_DIM
    assert tm % GRID_W == 0

    def rows(w):
        return pl.BlockSpec((tm, w), lambda i: (i, 0))

    lane_row = _resident((1, LANES))
    grid_rows = pl.BlockSpec((tm // GRID_W, LANES), lambda i: (i, 0))
    grid_cols = _resident((GRID_W, LANES))
    return pl.pallas_call(
        functools.partial(_inproj_kernel, widths=widths),
        grid=(s // tm,),
        in_specs=[rows(d), _resident((1, d)), _resident(win.shape), _resident(wv.shape), lane_row,
                  lane_row, grid_rows, grid_rows, grid_cols, grid_cols,
                  _resident(head_mean.shape), lane_row, lane_row, _resident((1, vw))],
        out_specs=[rows(qw * slots), pl.BlockSpec((kw * slots, tm), lambda i: (0, i)), rows(vw),
                   rows(fw), rows(gw), rows(gw)],
        out_shape=[jax.ShapeDtypeStruct((s, qw * slots), BF16),
                   jax.ShapeDtypeStruct((kw * slots, s), BF16),
                   jax.ShapeDtypeStruct((s, vw), BF16), jax.ShapeDtypeStruct((s, fw), BF16),
                   jax.ShapeDtypeStruct((s, gw), BF16), jax.ShapeDtypeStruct((s, gw), BF16)],
        compiler_params=_cparams(1),
        name="inproj",
    )(h, g, win, wv, qg, kg, *rope, head_mean, qx, kx, vone)


def _fft1_kernel(x_ref, cs1_ref, a_ref):
    n1, nb2, width = x_ref.shape
    x = x_ref[...]
    tiles = [x[:, :, t * LANES:(t + 1) * LANES].reshape(n1, nb2 * LANES) for t in range(width // LANES)]
    for j in range(nb2):
        xj = jnp.concatenate([t[:, j * LANES:(j + 1) * LANES] for t in tiles], axis=1)
        a_ref[:, j * width:(j + 1) * width] = _dot(cs1_ref[...], xj).astype(a_ref.dtype)


def _fft1(x3, cs1, *, nb2):
    n1, n2, width = x3.shape
    return pl.pallas_call(
        _fft1_kernel,
        grid=(n2 // nb2,),
        in_specs=[pl.BlockSpec((n1, nb2, width), lambda j: (0, j, 0)), _resident((2 * n1, n1))],
        out_specs=pl.BlockSpec((2 * n1, nb2 * width), lambda j: (0, j)),
        out_shape=jax.ShapeDtypeStruct((2 * n1, n2 * width), BF16),
        compiler_params=_cparams(1),
        name="fft1",
    )(x3, cs1)


def _fft2_kernel(ac_ref, as_ref, tc_ref, ts_ref, rot2_ref, mix_ref, o_ref):
    kb, n2, width = ac_ref.shape
    gd = FOURIER_GROUP_DIM
    per_k1 = []
    for kk in range(kb):
        ac = ac_ref[kk].astype(F32)
        as_ = as_ref[kk].astype(F32)
        tc = jnp.broadcast_to(tc_ref[0][:, kk:kk + 1], (n2, width))
        ts = jnp.broadcast_to(ts_ref[0][:, kk:kk + 1], (n2, width))
        b = jnp.concatenate([ac * tc - as_ * ts, as_ * tc + ac * ts], axis=0).astype(BF16)
        p = _dot(rot2_ref[...], b).astype(BF16)
        stacked = jnp.concatenate(
            [jnp.concatenate([p[:n2, g * gd:(g + 1) * gd], p[n2:, g * gd:(g + 1) * gd]], axis=1)
             for g in range(width // gd)], axis=0)
        per_k1.append(_dot(stacked, mix_ref[...]))
    for g in range(width // gd):
        side_by_side = jnp.concatenate([m[g * n2:(g + 1) * n2] for m in per_k1], axis=1)
        o_ref[:, :, g * gd:(g + 1) * gd] = side_by_side.reshape(n2, kb, gd).astype(o_ref.dtype)


def _fft2(a3, tc, ts, cs2, mix, *, kb):
    two_n1, n2, width = a3.shape
    n1 = two_n1 // 2
    nblk = n1 // kb
    return pl.pallas_call(
        _fft2_kernel,
        grid=(nblk,),
        in_specs=[pl.BlockSpec((kb, n2, width), lambda j: (j, 0, 0)),
                  pl.BlockSpec((kb, n2, width), lambda j: (j + nblk, 0, 0)),
                  pl.BlockSpec((1, n2, kb), lambda j: (j, 0, 0)),
                  pl.BlockSpec((1, n2, kb), lambda j: (j, 0, 0)),
                  _resident(cs2.shape), _resident(mix.shape)],
        out_specs=pl.BlockSpec((n2, kb, width), lambda j: (0, j, 0)),
        out_shape=jax.ShapeDtypeStruct((n2, n1, width), BF16),
        compiler_params=_cparams(1),
        name="fft2",
    )(a3, a3, tc, ts, cs2, mix)


def _dft_tables(s):
    n1 = s // FFT_INNER

    def cos_sin(n, scale):
        ang = 2.0 * np.pi * (np.outer(np.arange(n), np.arange(n)) % n) / n
        return np.cos(ang) * scale, np.sin(ang) * scale

    c1, s1 = cos_sin(n1, n1 ** -0.5)
    c2, s2 = cos_sin(FFT_INNER, FFT_INNER ** -0.5)
    cc, sc = cos_sin(FOURIER_GROUP_DIM, FOURIER_GROUP_DIM ** -0.5)
    as_bf16 = lambda a: jnp.asarray(a, dtype=F32).astype(BF16)
    return (as_bf16(np.concatenate([c1, s1], axis=0)), as_bf16(np.block([[c2, -s2], [s2, c2]])),
            as_bf16(np.concatenate([cc, -sc], axis=0)))


def _twiddles(s, kb):
    n1 = s // FFT_INNER
    ang = 2.0 * np.pi * (np.outer(np.arange(n1), np.arange(FFT_INNER)) % s) / s
    ang = ang.reshape(n1 // kb, kb, FFT_INNER).transpose(0, 2, 1)
    return jnp.asarray(np.cos(ang), dtype=F32), jnp.asarray(np.sin(ang), dtype=F32)


def _attn_kernel(q_ref, kt_ref, v_ref, *rest, tq, tk, online, side_blocks):
    n_side = len(side_blocks)
    o_ref = rest[n_side]
    step = pl.program_id(0) * pl.num_programs(1) + pl.program_id(1)
    for src, dst, nb in zip(rest[:n_side], rest[n_side + 1:], side_blocks):
        @pl.when(step < nb)
        def _(src=src, dst=dst):
            dst[...] = src[...].astype(dst.dtype)
    rows = GQA_GROUP * tq
    n_chunks = kt_ref.shape[1] // tk

    def query_block(b, carry):
        q_rows = pl.ds(pl.multiple_of(b * tq, tq), tq)
        qs = jnp.concatenate([q_ref[q_rows, g * LANES:(g + 1) * LANES] for g in range(GQA_GROUP)],
                             axis=0)
        acc0 = jnp.zeros((rows, LANES), F32)

        def scores(j):
            off = pl.multiple_of(j * tk, tk)
            return _dot(qs, kt_ref[:, pl.ds(off, tk)]), v_ref[pl.ds(off, tk), :]

        if online:
            def body(j, carry):
                m, acc = carry
                s, v = scores(j)
                m_new = jnp.maximum(m, jnp.max(s, axis=1, keepdims=True))
                p = jnp.exp2(s - m_new).astype(BF16)
                return m_new, acc * jnp.exp2(m - m_new) + _dot(p, v)

            m0 = jnp.full((rows, 1), -jnp.inf, F32)
            _, acc = lax.fori_loop(0, n_chunks, body, (m0, acc0))
        else:
            def body(j, acc):
                s, v = scores(j)
                return acc + _dot(jnp.exp2(s).astype(BF16), v)

            acc = lax.fori_loop(0, n_chunks, body, acc0)
        out = acc[:, :HEAD_DIM] / acc[:, HEAD_DIM:HEAD_DIM + 1]
        o_ref[q_rows, :] = jnp.concatenate([out[g * tq:(g + 1) * tq] for g in range(GQA_GROUP)],
                                           axis=1).astype(BF16)
        return carry

    lax.fori_loop(0, q_ref.shape[0] // tq, query_block, 0)


def _row_blocks(rows, max_blocks):
    for nb in range(min(max_blocks, rows // 16), 0, -1):
        if rows % nb == 0 and (rows // nb) % 16 == 0:
            return nb
    return 1


def _attn(q, kt, v, side, *, tq, q_per_step, tk, online):
    s = q.shape[0]
    gw = GQA_GROUP * HEAD_DIM
    tq_step = tq * q_per_step
    q_blocks = s // tq_step
    side_blocks = tuple(_row_blocks(w.shape[0], N_KV_HEADS * q_blocks) for w in side)
    side_specs = [pl.BlockSpec((w.shape[0] // nb, w.shape[1]),
                               lambda h, i, nb=nb: (jnp.minimum(h * q_blocks + i, nb - 1), 0))
                  for w, nb in zip(side, side_blocks)]
    outs = pl.pallas_call(
        functools.partial(_attn_kernel, tq=tq, tk=tk, online=online, side_blocks=side_blocks),
        grid=(N_KV_HEADS, q_blocks),
        in_specs=[pl.BlockSpec((tq_step, GQA_GROUP * LANES), lambda h, i: (i, h)),
                  pl.BlockSpec((LANES, s), lambda h, i: (h, 0)),
                  pl.BlockSpec((s, LANES), lambda h, i: (0, h))] + side_specs,
        out_specs=[pl.BlockSpec((tq_step, gw), lambda h, i: (i, h))] + side_specs,
        out_shape=[jax.ShapeDtypeStruct((s, N_KV_HEADS * gw), BF16)]
                  + [jax.ShapeDtypeStruct(w.shape, BF16) for w in side],
        compiler_params=_cparams(2),
        name="attn_online" if online else "attn",
    )(q, kt, v, *side)
    return outs[0], outs[1:]


def _memkv_kernel(mem_ref, g_ref, wk_ref, wv_ref, ktm_ref, vm_ref):
    mn = _rms(mem_ref[...], g_ref[...]).astype(BF16)
    kt = _dot(mn, wk_ref[...]).T
    v = _dot(mn, wv_ref[...])
    feat_row = lax.broadcasted_iota(jnp.int32, kt.shape, 0) // HEAD_DIM
    feat_col = lax.broadcasted_iota(jnp.int32, v.shape, 1) // HEAD_DIM
    for h in range(N_CROSS_HEADS):
        ktm_ref[h] = jnp.where(feat_row == h, kt, 0.0).astype(BF16)
        vm_ref[h] = jnp.where(feat_col == h, v, 0.0).astype(BF16)


def _memkv(mem, g, wk, wv):
    m, d = mem.shape
    cw = wk.shape[1]
    whole = lambda shape: pl.BlockSpec(shape, lambda: (0,) * len(shape))
    return pl.pallas_call(
        _memkv_kernel,
        in_specs=[whole((m, d)), whole((1, d)), whole((d, cw)), whole((d, cw))],
        out_specs=[whole((N_CROSS_HEADS, cw, m)), whole((N_CROSS_HEADS, m, cw))],
        out_shape=[jax.ShapeDtypeStruct((N_CROSS_HEADS, cw, m), BF16),
                   jax.ShapeDtypeStruct((N_CROSS_HEADS, m, cw), BF16)],
        compiler_params=pltpu.CompilerParams(vmem_limit_bytes=VMEM_LIMIT),
        name="memkv",
    )(mem, g, wk, wv)


def _merge_kernel(h_ref, fm_ref, o_ref, sf_ref, sa_ref, wfb_ref, wab_ref, wout_ref, gc_ref,
                  wcq_ref, ktm_ref, vm_ref, wco_ref, out_ref, *, sub_tiles):
    sub = h_ref.shape[0] // sub_tiles
    tiles = [slice(r * sub, (r + 1) * sub) for r in range(sub_tiles)]
    y_f = [_dot(fm_ref[t, :], wfb_ref[...]) for t in tiles]
    y_a = [_dot(o_ref[t, :], wab_ref[...]) for t in tiles]
    merged = [(sf_ref[t, :] * f + sa_ref[t, :] * a).astype(BF16) for t, f, a in zip(tiles, y_f, y_a)]
    h = [h_ref[t, :] + _dot(m, wout_ref[...]) for t, m in zip(tiles, merged)]
    n = [_rms(x, gc_ref[...]).astype(BF16) for x in h]
    qc = [(_dot(x, wcq_ref[...]) * (1.0 / math.sqrt(HEAD_DIM))).astype(BF16) for x in n]
    oc = [None] * sub_tiles
    for hh in range(N_CROSS_HEADS):
        s = [_dot(x, ktm_ref[hh]) for x in qc]
        e = [jnp.exp(x - jnp.max(x, axis=1, keepdims=True)) for x in s]
        p = [(x / jnp.sum(x, axis=1, keepdims=True)).astype(BF16) for x in e]
        head = [_dot(x, vm_ref[hh]) for x in p]
        oc = [x if acc is None else acc + x for acc, x in zip(oc, head)]
    for t, x, o in zip(tiles, h, oc):
        out_ref[t, :] = x + _dot(o.astype(BF16), wco_ref[...])


def _merge(h, fm, o, sf, sa, wfb, wab, wout, gc, wcq, ktm, vm, wco, *, tm, sub_tiles):
    s, d = h.shape

    def rows(w):
        return pl.BlockSpec((tm, w), lambda i: (i, 0))

    weights = [wfb, wab, wout, gc, wcq, ktm, vm, wco]
    return pl.pallas_call(
        functools.partial(_merge_kernel, sub_tiles=sub_tiles),
        grid=(s // tm,),
        in_specs=[rows(d), rows(fm.shape[1]), rows(o.shape[1]), rows(d), rows(d)]
                 + [_resident(w.shape) for w in weights],
        out_specs=rows(d),
        out_shape=jax.ShapeDtypeStruct((s, d), F32),
        compiler_params=_cparams(1),
        name="merge",
    )(h, fm, o, sf, sa, *weights)


def _rope_tables(s):
    f32 = np.float32
    lane = np.arange(LANES)
    inv = f32(1.0) / (f32(ROPE_THETA) ** (np.arange(0, ROPE_AXIS_DIM, 2, dtype=f32) / f32(ROPE_AXIS_DIM)))
    inv_lane = inv[lane % ROPE_HALF]
    row_lane = (lane % HEAD_DIM) < ROPE_AXIS_DIM
    sign = np.where((lane % ROPE_AXIS_DIM) < ROPE_HALF, f32(-1.0), f32(1.0))
    ang_r = np.arange(s // GRID_W, dtype=f32)[:, None] * inv_lane[None, :]
    ang_c = np.arange(GRID_W, dtype=f32)[:, None] * inv_lane[None, :]
    tables = (np.where(row_lane, np.cos(ang_r), 0.0), np.where(row_lane, np.sin(ang_r) * sign, 0.0),
              np.where(row_lane, 0.0, np.cos(ang_c)), np.where(row_lane, 0.0, np.sin(ang_c) * sign))
    return tuple(jnp.asarray(t, dtype=F32) for t in tables)


def _pick_tile(s, want):
    t = min(want, s)
    while s % t:
        t //= 2
    return t


def _tiles(s):
    tm = _pick_tile(s, 512)
    return dict(
        tokens=tm,
        ffn_tokens=_pick_tile(s, 2 * tm),
        ffn_hidden=3 * MXU_WIDTH,
        merge_tokens=_pick_tile(s, 2 * tm),
        merge_sub_tiles=4,
        fft_n2=16,
        fft_k1=min(16, s // FFT_INNER),
        attn_q=_pick_tile(s, 512),
        attn_q_per_step=max(1, min(2, s // _pick_tile(s, 512))),
        attn_kv=_pick_tile(s, 4096),
        attn_online_q=_pick_tile(s, 128),
        attn_online_kv=_pick_tile(s, 512),
    )


def kernel(x, mem, g_ffn1, w1_gate, w1_up, w1_down, g_mix, w_in, q_gain, k_gain, w_fourier_branch, w_attn_branch, w_out, g_cross, g_mem, w_cq, w_ck, w_cv, w_co, g_ffn2, w2_gate, w2_up, w2_down, g_final):
    batch, s, d = x.shape
    depth = w_in.shape[0]
    qw = w_attn_branch.shape[1]
    kw = N_KV_HEADS * HEAD_DIM
    fw = w_fourier_branch.shape[1]
    vw = N_KV_HEADS * LANES
    assert s % (FFT_INNER * 8) == 0 and s % GRID_W == 0
    assert qw == N_KV_HEADS * GQA_GROUP * HEAD_DIM and fw == FOURIER_GROUPS * FOURIER_GROUP_DIM
    n1 = s // FFT_INNER
    tiles = _tiles(s)
    row = lambda g: g.reshape(1, -1).astype(F32)
    bf = lambda w: w.astype(BF16)

    rope = _rope_tables(s)
    cs1, cs2, mix = _dft_tables(s)
    tw_c, tw_s = _twiddles(s, tiles["fft_k1"])
    lane = np.arange(LANES)
    mxu_lane = np.arange(MXU_WIDTH)
    head_mean = jnp.asarray((mxu_lane[:, None] // HEAD_DIM == mxu_lane[None, :] // HEAD_DIM) / HEAD_DIM,
                            dtype=F32).astype(BF16)
    vone = jnp.asarray((np.arange(vw) % LANES == HEAD_DIM).astype(np.float32)).reshape(1, vw)
    v_col = np.arange(kw)
    spread = np.zeros((kw, vw), np.float32)
    spread[v_col, v_col // HEAD_DIM * LANES + v_col % HEAD_DIM] = 1.0
    spread = jnp.asarray(spread)

    outs = []
    for b in range(batch):
        h = x[b]
        for l in range(depth):
            h = _ffn(h, row(g_ffn1[l]), bf(w1_gate[l]), bf(w1_up[l]), bf(w1_down[l]),
                     row(g_final), final_norm=False, tm=tiles["ffn_tokens"], dff_chunk=tiles["ffn_hidden"])

            wv = bf(jnp.dot(w_in[l][:, qw + kw:qw + 2 * kw], spread))
            shift = (LOG2E * math.sqrt(HEAD_DIM)) * jnp.max(jnp.abs(q_gain[l])) * jnp.max(jnp.abs(k_gain[l]))
            extra_lane = jnp.asarray((lane == HEAD_DIM).astype(np.float32)).reshape(1, LANES)
            q, kt, v, f, sf, sa = _inproj(
                h, row(g_mix[l]), bf(w_in[l]), wv, row(jnp.tile(q_gain[l], LANES // HEAD_DIM)),
                row(jnp.tile(k_gain[l], LANES // HEAD_DIM)), rope, head_mean,
                extra_lane, -shift * extra_lane, vone, widths=(qw, kw, fw, d), tm=tiles["tokens"])

            a = _fft1(f.reshape(n1, FFT_INNER, fw), cs1, nb2=tiles["fft_n2"])
            fm = _fft2(a.reshape(2 * n1, FFT_INNER, fw), tw_c, tw_s, cs2, mix, kb=tiles["fft_k1"])
            fm = fm.reshape(s, fw)

            later = (w_fourier_branch[l], w_attn_branch[l], w_out[l], w_cq[l], w_co[l], w_ck[l],
                     w_cv[l], w2_gate[l], w2_up[l], w2_down[l])
            o, (wfb, wab, wout, wcq, wco, wck, wcv, w2g, w2u, w2d) = lax.cond(
                shift <= MAX_SAFE_SHIFT,
                functools.partial(_attn, tq=tiles["attn_q"], q_per_step=tiles["attn_q_per_step"],
                                  tk=tiles["attn_kv"], online=False),
                functools.partial(_attn, tq=tiles["attn_online_q"], q_per_step=1,
                                  tk=tiles["attn_online_kv"], online=True),
                q, kt, v, later)

            ktm, vm = _memkv(mem[b], row(g_mem[l]), wck, wcv)
            h = _merge(h, fm, o, sf, sa, wfb, wab, wout, row(g_cross[l]), wcq, ktm, vm, wco,
                       tm=tiles["merge_tokens"], sub_tiles=tiles["merge_sub_tiles"])

            h = _ffn(h, row(g_ffn2[l]), w2g, w2u, w2d, row(g_final), final_norm=(l == depth - 1),
                     tm=tiles["ffn_tokens"], dff_chunk=tiles["ffn_hidden"])
        outs.append(h)
    return jnp.stack(outs, axis=0)
```

```python
import functools
import math

import numpy as np
import jax
import jax.numpy as jnp
from jax import lax
from jax.experimental import pallas as pl
from jax.experimental.pallas import tpu as pltpu

F32 = jnp.float32
BF16 = jnp.bfloat16

LANES = 128
MXU_WIDTH = 256
GRID_W = 64
HEAD_DIM = 64
N_KV_HEADS = 4
GQA_GROUP = 4
N_CROSS_HEADS = 4
FOURIER_GROUPS = 4
FOURIER_GROUP_DIM = 128
ROPE_AXIS_DIM = HEAD_DIM // 2
ROPE_HALF = ROPE_AXIS_DIM // 2
ROPE_THETA = 10000.0
RMS_EPS = 1e-6
FFT_INNER = 128
VMEM_LIMIT = 56 * 1024 * 1024
LOG2E = math.log2(math.e)
MAX_SAFE_SHIFT = 40.0


def _cparams(n_axes):
    return pltpu.CompilerParams(dimension_semantics=("arbitrary",) * n_axes,
                                vmem_limit_bytes=VMEM_LIMIT)


def _resident(shape):
    zeros = (0,) * len(shape)
    return pl.BlockSpec(shape, lambda *_: zeros, pipeline_mode=pl.Buffered(1))


def _dot(a, b):
    return jnp.dot(a, b, preferred_element_type=F32)


def _rms(x, g):
    y = x * lax.rsqrt(jnp.mean(x * x, axis=-1, keepdims=True) + RMS_EPS)
    return y * g


def _ffn_kernel(x_ref, g_ref, wg_ref, wu_ref, wd_ref, gfin_ref, o_ref, *, final_norm, dff_chunk):
    x = x_ref[...]
    xn = _rms(x, g_ref[...]).astype(BF16)
    dff = wg_ref.shape[1]
    acc = None
    for lo in range(0, dff, dff_chunk):
        hi = min(lo + dff_chunk, dff)
        gate = _dot(xn, wg_ref[:, lo:hi])
        up = _dot(xn, wu_ref[:, lo:hi])
        act = (gate * jax.nn.sigmoid(gate) * up).astype(BF16)
        part = _dot(act, wd_ref[lo:hi, :])
        acc = part if acc is None else acc + part
    h = x + 0.5 * acc
    if final_norm:
        h = _rms(h, gfin_ref[...])
    o_ref[...] = h


def _ffn(x, g, wg, wu, wd, gfin, *, final_norm, tm, dff_chunk):
    s, d = x.shape
    dff = wg.shape[1]
    row = pl.BlockSpec((tm, d), lambda i: (i, 0))
    return pl.pallas_call(
        functools.partial(_ffn_kernel, final_norm=final_norm, dff_chunk=dff_chunk),
        grid=(s // tm,),
        in_specs=[row, _resident((1, d)), _resident((d, dff)), _resident((d, dff)),
                  _resident((dff, d)), _resident((1, d))],
        out_specs=row,
        out_shape=jax.ShapeDtypeStruct((s, d), F32),
        compiler_params=_cparams(1),
        name="ffn_final" if final_norm else "ffn",
    )(x, g, wg, wu, wd, gfin)


def _head_norm_rope(x, gain, head_mean, cos, sin, first_half):
    ms = _dot((x * x).astype(BF16), head_mean)
    halves = []
    for c in range(x.shape[1] // LANES):
        sl = slice(c * LANES, (c + 1) * LANES)
        y = x[:, sl] * lax.rsqrt(ms[:, sl] + RMS_EPS) * gain
        partner = jnp.where(first_half,
                            pltpu.roll(y, LANES - ROPE_HALF, 1),
                            pltpu.roll(y, ROPE_HALF, 1))
        halves.append(y * cos + partner * sin)
    return halves


def _head_slots(xc, extra, low):
    return jnp.where(low, xc, extra), jnp.where(low, pltpu.roll(xc, HEAD_DIM, 1), extra)


def _token_table(row_ref, col_ref):
    col = col_ref[...]
    return jnp.concatenate([jnp.broadcast_to(row_ref[r:r + 1, :], col.shape) + col
                            for r in range(row_ref.shape[0])], axis=0)


def _inproj_kernel(h_ref, g_ref, win_ref, wv_ref, qg_ref, kg_ref, cosr_ref, sinr_ref, cosc_ref,
                   sinc_ref, hm_ref, qx_ref, kx_ref, vone_ref, q_ref, kt_ref, v_ref, f_ref, sf_ref,
                   sa_ref, *, widths):
    qw, kw, fw, gw = widths
    n = _rms(h_ref[...], g_ref[...]).astype(BF16)
    cos = _token_table(cosr_ref, cosc_ref)
    sin = _token_table(sinr_ref, sinc_ref)
    head_mean = hm_ref[...]
    lane = lax.broadcasted_iota(jnp.int32, cos.shape, 1)
    first_half = (lane % ROPE_AXIS_DIM) < ROPE_HALF
    low = lane < HEAD_DIM

    def proj(lo, width):
        return _dot(n, win_ref[:, lo:lo + width])

    q = proj(0, qw)
    pair = head_mean.shape[0]
    slot_lo = 0
    for c in range(qw // pair):
        for qc in _head_norm_rope(q[:, c * pair:(c + 1) * pair], qg_ref[...], head_mean, cos, sin,
                                  first_half):
            for slot in _head_slots(qc * (LOG2E / math.sqrt(HEAD_DIM)), qx_ref[...], low):
                q_ref[:, slot_lo:slot_lo + LANES] = slot.astype(BF16)
                slot_lo += LANES
    k = proj(qw, kw)
    k_slots = []
    for c in range(kw // pair):
        for kc in _head_norm_rope(k[:, c * pair:(c + 1) * pair], kg_ref[...], head_mean, cos, sin,
                                  first_half):
            k_slots.extend(_head_slots(kc, kx_ref[...], low))
    kt_ref[...] = jnp.concatenate(k_slots, axis=1).T.astype(BF16)
    v_ref[...] = (_dot(n, wv_ref[...]) + vone_ref[...]).astype(BF16)
    lo = qw + 2 * kw
    f_ref[...] = proj(lo, fw).astype(BF16)
    lo += fw
    sf_ref[...] = jax.nn.sigmoid(proj(lo, gw)).astype(sf_ref.dtype)
    lo += gw
    sa_ref[...] = jax.nn.sigmoid(proj(lo, gw)).astype(sa_ref.dtype)


def _inproj(h, g, win, wv, qg, kg, rope, head_mean, qx, kx, vone, *, widths, tm):
    s, d = h.shape
    qw, kw, fw, gw = widths
    vw = wv.shape[1]
    slots = LANES // HEAD_DIM
    assert tm % GRID_W == 0

    def rows(w):
        return pl.BlockSpec((tm, w), lambda i: (i, 0))

    lane_row = _resident((1, LANES))
    grid_rows = pl.BlockSpec((tm // GRID_W, LANES), lambda i: (i, 0))
    grid_cols = _resident((GRID_W, LANES))
    return pl.pallas_call(
        functools.partial(_inproj_kernel, widths=widths),
        grid=(s // tm,),
        in_specs=[rows(d), _resident((1, d)), _resident(win.shape), _resident(wv.shape), lane_row,
                  lane_row, grid_rows, grid_rows, grid_cols, grid_cols,
                  _resident(head_mean.shape), lane_row, lane_row, _resident((1, vw))],
        out_specs=[rows(qw * slots), pl.BlockSpec((kw * slots, tm), lambda i: (0, i)), rows(vw),
                   rows(fw), rows(gw), rows(gw)],
        out_shape=[jax.ShapeDtypeStruct((s, qw * slots), BF16),
                   jax.ShapeDtypeStruct((kw * slots, s), BF16),
                   jax.ShapeDtypeStruct((s, vw), BF16), jax.ShapeDtypeStruct((s, fw), BF16),
                   jax.ShapeDtypeStruct((s, gw), BF16), jax.ShapeDtypeStruct((s, gw), BF16)],
        compiler_params=_cparams(1),
        name="inproj",
    )(h, g, win, wv, qg, kg, *rope, head_mean, qx, kx, vone)


def _fft_stage1(step, x_ref, cs1_ref, a_ref):
    n1, nb2, width = x_ref.shape
    x = x_ref[...]
    tiles = [x[:, :, t * LANES:(t + 1) * LANES].reshape(n1, nb2 * LANES) for t in range(width // LANES)]
    res = [_dot(cs1_ref[...],
                jnp.concatenate([t[:, j * LANES:(j + 1) * LANES] for t in tiles], axis=1)).astype(a_ref.dtype)
           for j in range(nb2)]
    rows = pl.ds(pl.multiple_of(step * nb2, nb2), nb2)
    for t in range(width // LANES):
        side_by_side = jnp.concatenate([r[:, t * LANES:(t + 1) * LANES] for r in res], axis=1)
        a_ref[:, rows, t * LANES:(t + 1) * LANES] = side_by_side.reshape(2 * n1, nb2, LANES)


def _fft_stage2(blk, a_ref, tc_ref, ts_ref, rot2_ref, mix_ref, o_ref):
    n2, kb, width = o_ref.shape
    n1 = a_ref.shape[0] // 2
    gd = FOURIER_GROUP_DIM
    per_k1 = []
    for kk in range(kb):
        ac = a_ref[blk * kb + kk].astype(F32)
        as_ = a_ref[n1 + blk * kb + kk].astype(F32)
        tc = jnp.broadcast_to(tc_ref[0][:, kk:kk + 1], (n2, width))
        ts = jnp.broadcast_to(ts_ref[0][:, kk:kk + 1], (n2, width))
        b = jnp.concatenate([ac * tc - as_ * ts, as_ * tc + ac * ts], axis=0).astype(BF16)
        p = _dot(rot2_ref[...], b).astype(BF16)
        stacked = jnp.concatenate(
            [jnp.concatenate([p[:n2, g * gd:(g + 1) * gd], p[n2:, g * gd:(g + 1) * gd]], axis=1)
             for g in range(width // gd)], axis=0)
        per_k1.append(_dot(stacked, mix_ref[...]))
    for g in range(width // gd):
        side_by_side = jnp.concatenate([m[g * n2:(g + 1) * n2] for m in per_k1], axis=1)
        o_ref[:, :, g * gd:(g + 1) * gd] = side_by_side.reshape(n2, kb, gd).astype(o_ref.dtype)


def _fft_kernel(x_ref, cs1_ref, tc_ref, ts_ref, rot2_ref, mix_ref, o_ref, a_ref, *, stage1_steps):
    step = pl.program_id(0)

    @pl.when(step < stage1_steps)
    def _():
        _fft_stage1(step, x_ref, cs1_ref, a_ref)

    @pl.when(step >= stage1_steps)
    def _():
        _fft_stage2(step - stage1_steps, a_ref, tc_ref, ts_ref, rot2_ref, mix_ref, o_ref)


def _fft(x3, cs1, tc, ts, cs2, mix, *, nb2, kb):
    n1, n2, width = x3.shape
    stage1_steps = n2 // nb2
    stage2 = lambda j: jnp.maximum(j - stage1_steps, 0)
    return pl.pallas_call(
        functools.partial(_fft_kernel, stage1_steps=stage1_steps),
        grid=(stage1_steps + n1 // kb,),
        in_specs=[pl.BlockSpec((n1, nb2, width), lambda j: (0, jnp.minimum(j, stage1_steps - 1), 0)),
                  _resident(cs1.shape),
                  pl.BlockSpec((1, n2, kb), lambda j: (stage2(j), 0, 0)),
                  pl.BlockSpec((1, n2, kb), lambda j: (stage2(j), 0, 0)),
                  _resident(cs2.shape), _resident(mix.shape)],
        out_specs=pl.BlockSpec((n2, kb, width), lambda j: (0, stage2(j), 0)),
        out_shape=jax.ShapeDtypeStruct((n2, n1, width), BF16),
        scratch_shapes=[pltpu.VMEM((2 * n1, n2, width), BF16)],
        compiler_params=_cparams(1),
        name="fft",
    )(x3, cs1, tc, ts, cs2, mix)


def _dft_tables(s):
    n1 = s // FFT_INNER

    def cos_sin(n, scale):
        ang = 2.0 * np.pi * (np.outer(np.arange(n), np.arange(n)) % n) / n
        return np.cos(ang) * scale, np.sin(ang) * scale

    c1, s1 = cos_sin(n1, n1 ** -0.5)
    c2, s2 = cos_sin(FFT_INNER, FFT_INNER ** -0.5)
    cc, sc = cos_sin(FOURIER_GROUP_DIM, FOURIER_GROUP_DIM ** -0.5)
    as_bf16 = lambda a: jnp.asarray(a, dtype=F32).astype(BF16)
    return (as_bf16(np.concatenate([c1, s1], axis=0)), as_bf16(np.block([[c2, -s2], [s2, c2]])),
            as_bf16(np.concatenate([cc, -sc], axis=0)))


def _twiddles(s, kb):
    n1 = s // FFT_INNER
    ang = 2.0 * np.pi * (np.outer(np.arange(n1), np.arange(FFT_INNER)) % s) / s
    ang = ang.reshape(n1 // kb, kb, FFT_INNER).transpose(0, 2, 1)
    return jnp.asarray(np.cos(ang), dtype=F32), jnp.asarray(np.sin(ang), dtype=F32)


def _attn_kernel(q_ref, kt_ref, v_ref, *rest, tq, tk, online, side_blocks):
    n_side = len(side_blocks)
    o_ref = rest[n_side]
    step = pl.program_id(0) * pl.num_programs(1) + pl.program_id(1)
    for src, dst, nb in zip(rest[:n_side], rest[n_side + 1:], side_blocks):
        @pl.when(step < nb)
        def _(src=src, dst=dst):
            dst[...] = src[...].astype(dst.dtype)
    rows = GQA_GROUP * tq
    n_chunks = kt_ref.shape[1] // tk

    def query_block(b, carry):
        q_rows = pl.ds(pl.multiple_of(b * tq, tq), tq)
        qs = jnp.concatenate([q_ref[q_rows, g * LANES:(g + 1) * LANES] for g in range(GQA_GROUP)],
                             axis=0)
        acc0 = jnp.zeros((rows, LANES), F32)

        def scores(j):
            off = pl.multiple_of(j * tk, tk)
            return _dot(qs, kt_ref[:, pl.ds(off, tk)]), v_ref[pl.ds(off, tk), :]

        if online:
            def body(j, carry):
                m, acc = carry
                s, v = scores(j)
                m_new = jnp.maximum(m, jnp.max(s, axis=1, keepdims=True))
                p = jnp.exp2(s - m_new).astype(BF16)
                return m_new, acc * jnp.exp2(m - m_new) + _dot(p, v)

            m0 = jnp.full((rows, 1), -jnp.inf, F32)
            _, acc = lax.fori_loop(0, n_chunks, body, (m0, acc0))
        else:
            def body(j, acc):
                s, v = scores(j)
                return acc + _dot(jnp.exp2(s).astype(BF16), v)

            acc = lax.fori_loop(0, n_chunks, body, acc0)
        out = acc[:, :HEAD_DIM] / acc[:, HEAD_DIM:HEAD_DIM + 1]
        o_ref[q_rows, :] = jnp.concatenate([out[g * tq:(g + 1) * tq] for g in range(GQA_GROUP)],
                                           axis=1).astype(BF16)
        return carry

    lax.fori_loop(0, q_ref.shape[0] // tq, query_block, 0)


def _row_blocks(rows, max_blocks):
    for nb in range(min(max_blocks, rows // 16), 0, -1):
        if rows % nb == 0 and (rows // nb) % 16 == 0:
            return nb
    return 1


def _attn(q, kt, v, side, *, tq, q_per_step, tk, online):
    s = q.shape[0]
    gw = GQA_GROUP * HEAD_DIM
    tq_step = tq * q_per_step
    q_blocks = s // tq_step
    side_blocks = tuple(_row_blocks(w.shape[0], N_KV_HEADS * q_blocks) for w in side)
    side_specs = [pl.BlockSpec((w.shape[0] // nb, w.shape[1]),
                               lambda h, i, nb=nb: (jnp.minimum(h * q_blocks + i, nb - 1), 0))
                  for w, nb in zip(side, side_blocks)]
    outs = pl.pallas_call(
        functools.partial(_attn_kernel, tq=tq, tk=tk, online=online, side_blocks=side_blocks),
        grid=(N_KV_HEADS, q_blocks),
        in_specs=[pl.BlockSpec((tq_step, GQA_GROUP * LANES), lambda h, i: (i, h)),
                  pl.BlockSpec((LANES, s), lambda h, i: (h, 0)),
                  pl.BlockSpec((s, LANES), lambda h, i: (0, h))] + side_specs,
        out_specs=[pl.BlockSpec((tq_step, gw), lambda h, i: (i, h))] + side_specs,
        out_shape=[jax.ShapeDtypeStruct((s, N_KV_HEADS * gw), BF16)]
                  + [jax.ShapeDtypeStruct(w.shape, BF16) for w in side],
        compiler_params=_cparams(2),
        name="attn_online" if online else "attn",
    )(q, kt, v, *side)
    return outs[0], outs[1:]


def _memkv_kernel(mem_ref, g_ref, wk_ref, wv_ref, ktm_ref, vm_ref):
    mn = _rms(mem_ref[...], g_ref[...]).astype(BF16)
    kt = _dot(mn, wk_ref[...]).T
    v = _dot(mn, wv_ref[...])
    feat_row = lax.broadcasted_iota(jnp.int32, kt.shape, 0) // HEAD_DIM
    feat_col = lax.broadcasted_iota(jnp.int32, v.shape, 1) // HEAD_DIM
    for h in range(N_CROSS_HEADS):
        ktm_ref[h] = jnp.where(feat_row == h, kt, 0.0).astype(BF16)
        vm_ref[h] = jnp.where(feat_col == h, v, 0.0).astype(BF16)


def _memkv(mem, g, wk, wv):
    m, d = mem.shape
    cw = wk.shape[1]
    whole = lambda shape: pl.BlockSpec(shape, lambda: (0,) * len(shape))
    return pl.pallas_call(
        _memkv_kernel,
        in_specs=[whole((m, d)), whole((1, d)), whole((d, cw)), whole((d, cw))],
        out_specs=[whole((N_CROSS_HEADS, cw, m)), whole((N_CROSS_HEADS, m, cw))],
        out_shape=[jax.ShapeDtypeStruct((N_CROSS_HEADS, cw, m), BF16),
                   jax.ShapeDtypeStruct((N_CROSS_HEADS, m, cw), BF16)],
        compiler_params=pltpu.CompilerParams(vmem_limit_bytes=VMEM_LIMIT),
        name="memkv",
    )(mem, g, wk, wv)


def _merge_kernel(h_ref, fm_ref, o_ref, sf_ref, sa_ref, wfb_ref, wab_ref, wout_ref, gc_ref,
                  wcq_ref, ktm_ref, vm_ref, wco_ref, out_ref, *, sub_tiles):
    sub = h_ref.shape[0] // sub_tiles
    tiles = [slice(r * sub, (r + 1) * sub) for r in range(sub_tiles)]
    y_f = [_dot(fm_ref[t, :], wfb_ref[...]) for t in tiles]
    y_a = [_dot(o_ref[t, :], wab_ref[...]) for t in tiles]
    merged = [(sf_ref[t, :] * f + sa_ref[t, :] * a).astype(BF16) for t, f, a in zip(tiles, y_f, y_a)]
    h = [h_ref[t, :] + _dot(m, wout_ref[...]) for t, m in zip(tiles, merged)]
    n = [_rms(x, gc_ref[...]).astype(BF16) for x in h]
    qc = [(_dot(x, wcq_ref[...]) * (1.0 / math.sqrt(HEAD_DIM))).astype(BF16) for x in n]
    oc = [None] * sub_tiles
    for hh in range(N_CROSS_HEADS):
        s = [_dot(x, ktm_ref[hh]) for x in qc]
        e = [jnp.exp(x - jnp.max(x, axis=1, keepdims=True)) for x in s]
        p = [(x / jnp.sum(x, axis=1, keepdims=True)).astype(BF16) for x in e]
        head = [_dot(x, vm_ref[hh]) for x in p]
        oc = [x if acc is None else acc + x for acc, x in zip(oc, head)]
    for t, x, o in zip(tiles, h, oc):
        out_ref[t, :] = x + _dot(o.astype(BF16), wco_ref[...])


def _merge(h, fm, o, sf, sa, wfb, wab, wout, gc, wcq, ktm, vm, wco, *, tm, sub_tiles):
    s, d = h.shape

    def rows(w):
        return pl.BlockSpec((tm, w), lambda i: (i, 0))

    weights = [wfb, wab, wout, gc, wcq, ktm, vm, wco]
    return pl.pallas_call(
        functools.partial(_merge_kernel, sub_tiles=sub_tiles),
        grid=(s // tm,),
        in_specs=[rows(d), rows(fm.shape[1]), rows(o.shape[1]), rows(d), rows(d)]
                 + [_resident(w.shape) for w in weights],
        out_specs=rows(d),
        out_shape=jax.ShapeDtypeStruct((s, d), F32),
        compiler_params=_cparams(1),
        name="merge",
    )(h, fm, o, sf, sa, *weights)


def _rope_tables(s):
    f32 = np.float32
    lane = np.arange(LANES)
    inv = f32(1.0) / (f32(ROPE_THETA) ** (np.arange(0, ROPE_AXIS_DIM, 2, dtype=f32) / f32(ROPE_AXIS_DIM)))
    inv_lane = inv[lane % ROPE_HALF]
    row_lane = (lane % HEAD_DIM) < ROPE_AXIS_DIM
    sign = np.where((lane % ROPE_AXIS_DIM) < ROPE_HALF, f32(-1.0), f32(1.0))
    ang_r = np.arange(s // GRID_W, dtype=f32)[:, None] * inv_lane[None, :]
    ang_c = np.arange(GRID_W, dtype=f32)[:, None] * inv_lane[None, :]
    tables = (np.where(row_lane, np.cos(ang_r), 0.0), np.where(row_lane, np.sin(ang_r) * sign, 0.0),
              np.where(row_lane, 0.0, np.cos(ang_c)), np.where(row_lane, 0.0, np.sin(ang_c) * sign))
    return tuple(jnp.asarray(t, dtype=F32) for t in tables)


def _pick_tile(s, want):
    t = min(want, s)
    while s % t:
        t //= 2
    return t


def _tiles(s):
    tm = _pick_tile(s, 512)
    return dict(
        tokens=tm,
        ffn_tokens=_pick_tile(s, 2 * tm),
        ffn_hidden=3 * MXU_WIDTH,
        merge_tokens=_pick_tile(s, 2 * tm),
        merge_sub_tiles=4,
        fft_n2=16,
        fft_k1=min(16, s // FFT_INNER),
        attn_q=_pick_tile(s, 512),
        attn_q_per_step=max(1, min(2, s // _pick_tile(s, 512))),
        attn_kv=_pick_tile(s, 4096),
        attn_online_q=_pick_tile(s, 128),
        attn_online_kv=_pick_tile(s, 512),
    )


def kernel(x, mem, g_ffn1, w1_gate, w1_up, w1_down, g_mix, w_in, q_gain, k_gain, w_fourier_branch, w_attn_branch, w_out, g_cross, g_mem, w_cq, w_ck, w_cv, w_co, g_ffn2, w2_gate, w2_up, w2_down, g_final):
    batch, s, d = x.shape
    depth = w_in.shape[0]
    qw = w_attn_branch.shape[1]
    kw = N_KV_HEADS * HEAD_DIM
    fw = w_fourier_branch.shape[1]
    vw = N_KV_HEADS * LANES
    assert s % (FFT_INNER * 8) == 0 and s % GRID_W == 0
    assert qw == N_KV_HEADS * GQA_GROUP * HEAD_DIM and fw == FOURIER_GROUPS * FOURIER_GROUP_DIM
    n1 = s // FFT_INNER
    tiles = _tiles(s)
    row = lambda g: g.reshape(1, -1).astype(F32)
    bf = lambda w: w.astype(BF16)

    rope = _rope_tables(s)
    cs1, cs2, mix = _dft_tables(s)
    tw_c, tw_s = _twiddles(s, tiles["fft_k1"])
    lane = np.arange(LANES)
    mxu_lane = np.arange(MXU_WIDTH)
    head_mean = jnp.asarray((mxu_lane[:, None] // HEAD_DIM == mxu_lane[None, :] // HEAD_DIM) / HEAD_DIM,
                            dtype=F32).astype(BF16)
    vone = jnp.asarray((np.arange(vw) % LANES == HEAD_DIM).astype(np.float32)).reshape(1, vw)
    v_col = np.arange(kw)
    spread = np.zeros((kw, vw), np.float32)
    spread[v_col, v_col // HEAD_DIM * LANES + v_col % HEAD_DIM] = 1.0
    spread = jnp.asarray(spread)

    outs = []
    for b in range(batch):
        h = x[b]
        for l in range(depth):
            h = _ffn(h, row(g_ffn1[l]), bf(w1_gate[l]), bf(w1_up[l]), bf(w1_down[l]),
                     row(g_final), final_norm=False, tm=tiles["ffn_tokens"], dff_chunk=tiles["ffn_hidden"])

            wv = bf(jnp.dot(w_in[l][:, qw + kw:qw + 2 * kw], spread))
            shift = (LOG2E * math.sqrt(HEAD_DIM)) * jnp.max(jnp.abs(q_gain[l])) * jnp.max(jnp.abs(k_gain[l]))
            extra_lane = jnp.asarray((lane == HEAD_DIM).astype(np.float32)).reshape(1, LANES)
            q, kt, v, f, sf, sa = _inproj(
                h, row(g_mix[l]), bf(w_in[l]), wv, row(jnp.tile(q_gain[l], LANES // HEAD_DIM)),
                row(jnp.tile(k_gain[l], LANES // HEAD_DIM)), rope, head_mean,
                extra_lane, -shift * extra_lane, vone, widths=(qw, kw, fw, d), tm=tiles["tokens"])

            fm = _fft(f.reshape(n1, FFT_INNER, fw), cs1, tw_c, tw_s, cs2, mix,
                      nb2=tiles["fft_n2"], kb=tiles["fft_k1"])
            fm = fm.reshape(s, fw)

            later = (w_fourier_branch[l], w_attn_branch[l], w_out[l], w_cq[l], w_co[l], w_ck[l],
                     w_cv[l], w2_gate[l], w2_up[l], w2_down[l])
            o, (wfb, wab, wout, wcq, wco, wck, wcv, w2g, w2u, w2d) = lax.cond(
                shift <= MAX_SAFE_SHIFT,
                functools.partial(_attn, tq=tiles["attn_q"], q_per_step=tiles["attn_q_per_step"],
                                  tk=tiles["attn_kv"], online=False),
                functools.partial(_attn, tq=tiles["attn_online_q"], q_per_step=1,
                                  tk=tiles["attn_online_kv"], online=True),
                q, kt, v, later)

            ktm, vm = _memkv(mem[b], row(g_mem[l]), wck, wcv)
            h = _merge(h, fm, o, sf, sa, wfb, wab, wout, row(g_cross[l]), wcq, ktm, vm, wco,
                       tm=tiles["merge_tokens"], sub_tiles=tiles["merge_sub_tiles"])

            h = _ffn(h, row(g_ffn2[l]), w2g, w2u, w2d, row(g_final), final_norm=(l == depth - 1),
                     tm=tiles["ffn_tokens"], dff_chunk=tiles["ffn_hidden"])
        outs.append(h)
    return jnp.stack(outs, axis=0)
```

```python
import functools
import math

import numpy as np
import jax
import jax.numpy as jnp
from jax import lax
from jax.experimental import pallas as pl
from jax.experimental.pallas import tpu as pltpu

F32 = jnp.float32
BF16 = jnp.bfloat16

LANES = 128
MXU_WIDTH = 256
GRID_W = 64
HEAD_DIM = 64
N_KV_HEADS = 4
GQA_GROUP = 4
N_CROSS_HEADS = 4
FOURIER_GROUPS = 4
FOURIER_GROUP_DIM = 128
ROPE_AXIS_DIM = HEAD_DIM // 2
ROPE_HALF = ROPE_AXIS_DIM // 2
ROPE_THETA = 10000.0
RMS_EPS = 1e-6
FFT_INNER = 128
VMEM_LIMIT = 56 * 1024 * 1024
LOG2E = math.log2(math.e)
MAX_SAFE_SHIFT = 40.0


def _cparams(n_axes):
    return pltpu.CompilerParams(dimension_semantics=("arbitrary",) * n_axes,
                                vmem_limit_bytes=VMEM_LIMIT)


def _resident(shape):
    zeros = (0,) * len(shape)
    return pl.BlockSpec(shape, lambda *_: zeros, pipeline_mode=pl.Buffered(1))


def _dot(a, b):
    return jnp.dot(a, b, preferred_element_type=F32)


def _rms(x, g):
    y = x * lax.rsqrt(jnp.mean(x * x, axis=-1, keepdims=True) + RMS_EPS)
    return y * g


def _ffn_kernel(x_ref, g_ref, wg_ref, wu_ref, wd_ref, gfin_ref, o_ref, *, final_norm, dff_chunk):
    x = x_ref[...]
    xn = _rms(x, g_ref[...]).astype(BF16)
    dff = wg_ref.shape[1]
    acc = None
    for lo in range(0, dff, dff_chunk):
        hi = min(lo + dff_chunk, dff)
        gate = _dot(xn, wg_ref[:, lo:hi])
        up = _dot(xn, wu_ref[:, lo:hi])
        act = (gate * jax.nn.sigmoid(gate) * up).astype(BF16)
        part = _dot(act, wd_ref[lo:hi, :])
        acc = part if acc is None else acc + part
    h = x + 0.5 * acc
    if final_norm:
        h = _rms(h, gfin_ref[...])
    o_ref[...] = h


def _ffn(x, g, wg, wu, wd, gfin, *, final_norm, tm, dff_chunk):
    s, d = x.shape
    dff = wg.shape[1]
    row = pl.BlockSpec((tm, d), lambda i: (i, 0))
    return pl.pallas_call(
        functools.partial(_ffn_kernel, final_norm=final_norm, dff_chunk=dff_chunk),
        grid=(s // tm,),
        in_specs=[row, _resident((1, d)), _resident((d, dff)), _resident((d, dff)),
                  _resident((dff, d)), _resident((1, d))],
        out_specs=row,
        out_shape=jax.ShapeDtypeStruct((s, d), F32),
        compiler_params=_cparams(1),
        name="ffn_final" if final_norm else "ffn",
    )(x, g, wg, wu, wd, gfin)


def _head_norm_rope(x, gain, head_mean, cos, sin, first_half):
    ms = _dot((x * x).astype(BF16), head_mean)
    halves = []
    for c in range(x.shape[1] // LANES):
        sl = slice(c * LANES, (c + 1) * LANES)
        y = x[:, sl] * lax.rsqrt(ms[:, sl] + RMS_EPS) * gain
        partner = jnp.where(first_half,
                            pltpu.roll(y, LANES - ROPE_HALF, 1),
                            pltpu.roll(y, ROPE_HALF, 1))
        halves.append(y * cos + partner * sin)
    return halves


def _head_slots(xc, extra, low):
    return jnp.where(low, xc, extra), jnp.where(low, pltpu.roll(xc, HEAD_DIM, 1), extra)


def _token_table(row_ref, col_ref):
    col = col_ref[...]
    return jnp.concatenate([jnp.broadcast_to(row_ref[r:r + 1, :], col.shape) + col
                            for r in range(row_ref.shape[0])], axis=0)


def _inproj_kernel(h_ref, g_ref, win_ref, wv_ref, qg_ref, kg_ref, cosr_ref, sinr_ref, cosc_ref,
                   sinc_ref, hm_ref, qx_ref, kx_ref, vone_ref, q_ref, kt_ref, v_ref, f_ref, sf_ref,
                   sa_ref, *, widths):
    qw, kw, fw, gw = widths
    n = _rms(h_ref[...], g_ref[...]).astype(BF16)
    cos = _token_table(cosr_ref, cosc_ref)
    sin = _token_table(sinr_ref, sinc_ref)
    head_mean = hm_ref[...]
    lane = lax.broadcasted_iota(jnp.int32, cos.shape, 1)
    first_half = (lane % ROPE_AXIS_DIM) < ROPE_HALF
    low = lane < HEAD_DIM

    def proj(lo, width):
        return _dot(n, win_ref[:, lo:lo + width])

    q = proj(0, qw)
    pair = head_mean.shape[0]
    slot_lo = 0
    for c in range(qw // pair):
        for qc in _head_norm_rope(q[:, c * pair:(c + 1) * pair], qg_ref[...], head_mean, cos, sin,
                                  first_half):
            for slot in _head_slots(qc * (LOG2E / math.sqrt(HEAD_DIM)), qx_ref[...], low):
                q_ref[:, slot_lo:slot_lo + LANES] = slot.astype(BF16)
                slot_lo += LANES
    k = proj(qw, kw)
    k_slots = []
    for c in range(kw // pair):
        for kc in _head_norm_rope(k[:, c * pair:(c + 1) * pair], kg_ref[...], head_mean, cos, sin,
                                  first_half):
            k_slots.extend(_head_slots(kc, kx_ref[...], low))
    kt_ref[...] = jnp.concatenate(k_slots, axis=1).T.astype(BF16)
    v_ref[...] = (_dot(n, wv_ref[...]) + vone_ref[...]).astype(BF16)
    lo = qw + 2 * kw
    f_ref[...] = proj(lo, fw).astype(BF16)
    lo += fw
    sf_ref[...] = jax.nn.sigmoid(proj(lo, gw)).astype(sf_ref.dtype)
    lo += gw
    sa_ref[...] = jax.nn.sigmoid(proj(lo, gw)).astype(sa_ref.dtype)


def _inproj(h, g, win, wv, qg, kg, rope, head_mean, qx, kx, vone, *, widths, tm):
    s, d = h.shape
    qw, kw, fw, gw = widths
    vw = wv.shape[1]
    slots = LANES // HEAD_DIM
    assert tm % GRID_W == 0

    def rows(w):
        return pl.BlockSpec((tm, w), lambda i: (i, 0))

    lane_row = _resident((1, LANES))
    grid_rows = pl.BlockSpec((tm // GRID_W, LANES), lambda i: (i, 0))
    grid_cols = _resident((GRID_W, LANES))
    return pl.pallas_call(
        functools.partial(_inproj_kernel, widths=widths),
        grid=(s // tm,),
        in_specs=[rows(d), _resident((1, d)), _resident(win.shape), _resident(wv.shape), lane_row,
                  lane_row, grid_rows, grid_rows, grid_cols, grid_cols,
                  _resident(head_mean.shape), lane_row, lane_row, _resident((1, vw))],
        out_specs=[rows(qw * slots), pl.BlockSpec((kw * slots, tm), lambda i: (0, i)), rows(vw),
                   rows(fw), rows(gw), rows(gw)],
        out_shape=[jax.ShapeDtypeStruct((s, qw * slots), BF16),
                   jax.ShapeDtypeStruct((kw * slots, s), BF16),
                   jax.ShapeDtypeStruct((s, vw), BF16), jax.ShapeDtypeStruct((s, fw), BF16),
                   jax.ShapeDtypeStruct((s, gw), BF16), jax.ShapeDtypeStruct((s, gw), BF16)],
        compiler_params=_cparams(1),
        name="inproj",
    )(h, g, win, wv, qg, kg, *rope, head_mean, qx, kx, vone)


def _fft_stage1(step, x_ref, cs1_ref, a_ref):
    n1, nb2, width = x_ref.shape
    x = x_ref[...]
    tiles = [x[:, :, t * LANES:(t + 1) * LANES].reshape(n1, nb2 * LANES) for t in range(width // LANES)]
    res = [_dot(cs1_ref[...],
                jnp.concatenate([t[:, j * LANES:(j + 1) * LANES] for t in tiles], axis=1)).astype(a_ref.dtype)
           for j in range(nb2)]
    rows = pl.ds(pl.multiple_of(step * nb2, nb2), nb2)
    for t in range(width // LANES):
        side_by_side = jnp.concatenate([r[:, t * LANES:(t + 1) * LANES] for r in res], axis=1)
        a_ref[:, rows, t * LANES:(t + 1) * LANES] = side_by_side.reshape(2 * n1, nb2, LANES)


def _fft_stage2(blk, a_ref, tc_ref, ts_ref, rot2_ref, mix_ref, o_ref):
    n2, kb, width = o_ref.shape
    n1 = a_ref.shape[0] // 2
    gd = FOURIER_GROUP_DIM
    per_k1 = []
    for kk in range(kb):
        ac = a_ref[blk * kb + kk].astype(F32)
        as_ = a_ref[n1 + blk * kb + kk].astype(F32)
        tc = jnp.broadcast_to(tc_ref[0][:, kk:kk + 1], (n2, width))
        ts = jnp.broadcast_to(ts_ref[0][:, kk:kk + 1], (n2, width))
        b = jnp.concatenate([ac * tc - as_ * ts, as_ * tc + ac * ts], axis=0).astype(BF16)
        p = _dot(rot2_ref[...], b).astype(BF16)
        stacked = jnp.concatenate(
            [jnp.concatenate([p[:n2, g * gd:(g + 1) * gd], p[n2:, g * gd:(g + 1) * gd]], axis=1)
             for g in range(width // gd)], axis=0)
        per_k1.append(_dot(stacked, mix_ref[...]))
    for g in range(width // gd):
        side_by_side = jnp.concatenate([m[g * n2:(g + 1) * n2] for m in per_k1], axis=1)
        o_ref[:, :, g * gd:(g + 1) * gd] = side_by_side.reshape(n2, kb, gd).astype(o_ref.dtype)


def _fft_kernel(x_ref, cs1_ref, tc_ref, ts_ref, rot2_ref, mix_ref, o_ref, a_ref, *, stage1_steps):
    step = pl.program_id(0)

    @pl.when(step < stage1_steps)
    def _():
        _fft_stage1(step, x_ref, cs1_ref, a_ref)

    @pl.when(step >= stage1_steps)
    def _():
        _fft_stage2(step - stage1_steps, a_ref, tc_ref, ts_ref, rot2_ref, mix_ref, o_ref)


def _fft(x3, cs1, tc, ts, cs2, mix, *, nb2, kb):
    n1, n2, width = x3.shape
    stage1_steps = n2 // nb2
    stage2 = lambda j: jnp.maximum(j - stage1_steps, 0)
    return pl.pallas_call(
        functools.partial(_fft_kernel, stage1_steps=stage1_steps),
        grid=(stage1_steps + n1 // kb,),
        in_specs=[pl.BlockSpec((n1, nb2, width), lambda j: (0, jnp.minimum(j, stage1_steps - 1), 0)),
                  _resident(cs1.shape),
                  pl.BlockSpec((1, n2, kb), lambda j: (stage2(j), 0, 0)),
                  pl.BlockSpec((1, n2, kb), lambda j: (stage2(j), 0, 0)),
                  _resident(cs2.shape), _resident(mix.shape)],
        out_specs=pl.BlockSpec((n2, kb, width), lambda j: (0, stage2(j), 0)),
        out_shape=jax.ShapeDtypeStruct((n2, n1, width), BF16),
        scratch_shapes=[pltpu.VMEM((2 * n1, n2, width), BF16)],
        compiler_params=_cparams(1),
        name="fft",
    )(x3, cs1, tc, ts, cs2, mix)


def _dft_tables(s):
    n1 = s // FFT_INNER

    def cos_sin(n, scale):
        ang = 2.0 * np.pi * (np.outer(np.arange(n), np.arange(n)) % n) / n
        return np.cos(ang) * scale, np.sin(ang) * scale

    c1, s1 = cos_sin(n1, n1 ** -0.5)
    c2, s2 = cos_sin(FFT_INNER, FFT_INNER ** -0.5)
    cc, sc = cos_sin(FOURIER_GROUP_DIM, FOURIER_GROUP_DIM ** -0.5)
    as_bf16 = lambda a: jnp.asarray(a, dtype=F32).astype(BF16)
    return (as_bf16(np.concatenate([c1, s1], axis=0)), as_bf16(np.block([[c2, -s2], [s2, c2]])),
            as_bf16(np.concatenate([cc, -sc], axis=0)))


def _twiddles(s, kb):
    n1 = s // FFT_INNER
    ang = 2.0 * np.pi * (np.outer(np.arange(n1), np.arange(FFT_INNER)) % s) / s
    ang = ang.reshape(n1 // kb, kb, FFT_INNER).transpose(0, 2, 1)
    return jnp.asarray(np.cos(ang), dtype=F32), jnp.asarray(np.sin(ang), dtype=F32)


def _attn_kernel(q_ref, kt_ref, v_ref, *rest, tq, tk, online, side_blocks):
    n_side = len(side_blocks)
    o_ref = rest[n_side]
    step = pl.program_id(0) * pl.num_programs(1) + pl.program_id(1)
    for src, dst, nb in zip(rest[:n_side], rest[n_side + 1:], side_blocks):
        @pl.when(step < nb)
        def _(src=src, dst=dst):
            dst[...] = src[...].astype(dst.dtype)
    rows = GQA_GROUP * tq
    n_chunks = kt_ref.shape[1] // tk

    def query_block(b, carry):
        q_rows = pl.ds(pl.multiple_of(b * tq, tq), tq)
        qs = jnp.concatenate([q_ref[q_rows, g * LANES:(g + 1) * LANES] for g in range(GQA_GROUP)],
                             axis=0)
        acc0 = jnp.zeros((rows, LANES), F32)

        def scores(j):
            off = pl.multiple_of(j * tk, tk)
            return _dot(qs, kt_ref[:, pl.ds(off, tk)]), v_ref[pl.ds(off, tk), :]

        if online:
            def body(j, carry):
                m, acc = carry
                s, v = scores(j)
                m_new = jnp.maximum(m, jnp.max(s, axis=1, keepdims=True))
                p = jnp.exp2(s - m_new).astype(BF16)
                return m_new, acc * jnp.exp2(m - m_new) + _dot(p, v)

            m0 = jnp.full((rows, 1), -jnp.inf, F32)
            _, acc = lax.fori_loop(0, n_chunks, body, (m0, acc0))
        else:
            def body(j, acc):
                s, v = scores(j)
                return acc + _dot(jnp.exp2(s).astype(BF16), v)

            acc = lax.fori_loop(0, n_chunks, body, acc0)
        out = acc[:, :HEAD_DIM] / acc[:, HEAD_DIM:HEAD_DIM + 1]
        o_ref[q_rows, :] = jnp.concatenate([out[g * tq:(g + 1) * tq] for g in range(GQA_GROUP)],
                                           axis=1).astype(BF16)
        return carry

    lax.fori_loop(0, q_ref.shape[0] // tq, query_block, 0)


def _row_blocks(rows, max_blocks):
    for nb in range(min(max_blocks, rows // 16), 0, -1):
        if rows % nb == 0 and (rows // nb) % 16 == 0:
            return nb
    return 1


def _attn(q, kt, v, side, *, tq, q_per_step, tk, online):
    s = q.shape[0]
    gw = GQA_GROUP * HEAD_DIM
    tq_step = tq * q_per_step
    q_blocks = s // tq_step
    side_blocks = tuple(_row_blocks(w.shape[0], N_KV_HEADS * q_blocks) for w in side)
    side_specs = [pl.BlockSpec((w.shape[0] // nb, w.shape[1]),
                               lambda h, i, nb=nb: (jnp.minimum(h * q_blocks + i, nb - 1), 0))
                  for w, nb in zip(side, side_blocks)]
    outs = pl.pallas_call(
        functools.partial(_attn_kernel, tq=tq, tk=tk, online=online, side_blocks=side_blocks),
        grid=(N_KV_HEADS, q_blocks),
        in_specs=[pl.BlockSpec((tq_step, GQA_GROUP * LANES), lambda h, i: (i, h)),
                  pl.BlockSpec((LANES, s), lambda h, i: (h, 0)),
                  pl.BlockSpec((s, LANES), lambda h, i: (0, h))] + side_specs,
        out_specs=[pl.BlockSpec((tq_step, gw), lambda h, i: (i, h))] + side_specs,
        out_shape=[jax.ShapeDtypeStruct((s, N_KV_HEADS * gw), BF16)]
                  + [jax.ShapeDtypeStruct(w.shape, BF16) for w in side],
        compiler_params=_cparams(2),
        name="attn_online" if online else "attn",
    )(q, kt, v, *side)
    return outs[0], outs[1:]


def _memkv(mem_ref, g_ref, wk_ref, wv_ref, ktm_ref, vm_ref):
    mn = _rms(mem_ref[...], g_ref[...]).astype(BF16)
    kt = _dot(mn, wk_ref[...]).T
    v = _dot(mn, wv_ref[...])
    feat_row = lax.broadcasted_iota(jnp.int32, kt.shape, 0) // HEAD_DIM
    feat_col = lax.broadcasted_iota(jnp.int32, v.shape, 1) // HEAD_DIM
    for h in range(N_CROSS_HEADS):
        ktm_ref[h] = jnp.where(feat_row == h, kt, 0.0).astype(BF16)
        vm_ref[h] = jnp.where(feat_col == h, v, 0.0).astype(BF16)


def _merge_kernel(h_ref, fm_ref, o_ref, sf_ref, sa_ref, wfb_ref, wab_ref, wout_ref, gc_ref,
                  wcq_ref, wco_ref, mem_ref, gm_ref, wck_ref, wcv_ref, out_ref, ktm_ref, vm_ref,
                  *, sub_tiles):
    @pl.when(pl.program_id(0) == 0)
    def _():
        _memkv(mem_ref, gm_ref, wck_ref, wcv_ref, ktm_ref, vm_ref)

    sub = h_ref.shape[0] // sub_tiles
    tiles = [slice(r * sub, (r + 1) * sub) for r in range(sub_tiles)]
    y_f = [_dot(fm_ref[t, :], wfb_ref[...]) for t in tiles]
    y_a = [_dot(o_ref[t, :], wab_ref[...]) for t in tiles]
    merged = [(sf_ref[t, :] * f + sa_ref[t, :] * a).astype(BF16) for t, f, a in zip(tiles, y_f, y_a)]
    h = [h_ref[t, :] + _dot(m, wout_ref[...]) for t, m in zip(tiles, merged)]
    n = [_rms(x, gc_ref[...]).astype(BF16) for x in h]
    qc = [(_dot(x, wcq_ref[...]) * (1.0 / math.sqrt(HEAD_DIM))).astype(BF16) for x in n]
    oc = [None] * sub_tiles
    for hh in range(N_CROSS_HEADS):
        s = [_dot(x, ktm_ref[hh]) for x in qc]
        e = [jnp.exp(x - jnp.max(x, axis=1, keepdims=True)) for x in s]
        p = [(x / jnp.sum(x, axis=1, keepdims=True)).astype(BF16) for x in e]
        head = [_dot(x, vm_ref[hh]) for x in p]
        oc = [x if acc is None else acc + x for acc, x in zip(oc, head)]
    for t, x, o in zip(tiles, h, oc):
        out_ref[t, :] = x + _dot(o.astype(BF16), wco_ref[...])


def _merge(h, fm, o, sf, sa, wfb, wab, wout, gc, wcq, wco, mem, gm, wck, wcv, *, tm, sub_tiles):
    s, d = h.shape
    m, cw = mem.shape[0], wck.shape[1]

    def rows(w):
        return pl.BlockSpec((tm, w), lambda i: (i, 0))

    weights = [wfb, wab, wout, gc, wcq, wco, mem, gm, wck, wcv]
    return pl.pallas_call(
        functools.partial(_merge_kernel, sub_tiles=sub_tiles),
        grid=(s // tm,),
        in_specs=[rows(d), rows(fm.shape[1]), rows(o.shape[1]), rows(d), rows(d)]
                 + [_resident(w.shape) for w in weights],
        out_specs=rows(d),
        out_shape=jax.ShapeDtypeStruct((s, d), F32),
        scratch_shapes=[pltpu.VMEM((N_CROSS_HEADS, cw, m), BF16),
                        pltpu.VMEM((N_CROSS_HEADS, m, cw), BF16)],
        compiler_params=_cparams(1),
        name="merge",
    )(h, fm, o, sf, sa, *weights)


def _rope_tables(s):
    f32 = np.float32
    lane = np.arange(LANES)
    inv = f32(1.0) / (f32(ROPE_THETA) ** (np.arange(0, ROPE_AXIS_DIM, 2, dtype=f32) / f32(ROPE_AXIS_DIM)))
    inv_lane = inv[lane % ROPE_HALF]
    row_lane = (lane % HEAD_DIM) < ROPE_AXIS_DIM
    sign = np.where((lane % ROPE_AXIS_DIM) < ROPE_HALF, f32(-1.0), f32(1.0))
    ang_r = np.arange(s // GRID_W, dtype=f32)[:, None] * inv_lane[None, :]
    ang_c = np.arange(GRID_W, dtype=f32)[:, None] * inv_lane[None, :]
    tables = (np.where(row_lane, np.cos(ang_r), 0.0), np.where(row_lane, np.sin(ang_r) * sign, 0.0),
              np.where(row_lane, 0.0, np.cos(ang_c)), np.where(row_lane, 0.0, np.sin(ang_c) * sign))
    return tuple(jnp.asarray(t, dtype=F32) for t in tables)


def _pick_tile(s, want):
    t = min(want, s)
    while s % t:
        t //= 2
    return t


def _tiles(s):
    tm = _pick_tile(s, 512)
    return dict(
        tokens=tm,
        ffn_tokens=_pick_tile(s, 2 * tm),
        ffn_hidden=3 * MXU_WIDTH,
        merge_tokens=_pick_tile(s, 2 * tm),
        merge_sub_tiles=4,
        fft_n2=16,
        fft_k1=min(16, s // FFT_INNER),
        attn_q=_pick_tile(s, 512),
        attn_q_per_step=max(1, min(2, s // _pick_tile(s, 512))),
        attn_kv=_pick_tile(s, 4096),
        attn_online_q=_pick_tile(s, 128),
        attn_online_kv=_pick_tile(s, 512),
    )


def kernel(x, mem, g_ffn1, w1_gate, w1_up, w1_down, g_mix, w_in, q_gain, k_gain, w_fourier_branch, w_attn_branch, w_out, g_cross, g_mem, w_cq, w_ck, w_cv, w_co, g_ffn2, w2_gate, w2_up, w2_down, g_final):
    batch, s, d = x.shape
    depth = w_in.shape[0]
    qw = w_attn_branch.shape[1]
    kw = N_KV_HEADS * HEAD_DIM
    fw = w_fourier_branch.shape[1]
    vw = N_KV_HEADS * LANES
    assert s % (FFT_INNER * 8) == 0 and s % GRID_W == 0
    assert qw == N_KV_HEADS * GQA_GROUP * HEAD_DIM and fw == FOURIER_GROUPS * FOURIER_GROUP_DIM
    n1 = s // FFT_INNER
    tiles = _tiles(s)
    row = lambda g: g.reshape(1, -1).astype(F32)
    bf = lambda w: w.astype(BF16)

    rope = _rope_tables(s)
    cs1, cs2, mix = _dft_tables(s)
    tw_c, tw_s = _twiddles(s, tiles["fft_k1"])
    lane = np.arange(LANES)
    mxu_lane = np.arange(MXU_WIDTH)
    head_mean = jnp.asarray((mxu_lane[:, None] // HEAD_DIM == mxu_lane[None, :] // HEAD_DIM) / HEAD_DIM,
                            dtype=F32).astype(BF16)
    vone = jnp.asarray((np.arange(vw) % LANES == HEAD_DIM).astype(np.float32)).reshape(1, vw)
    v_col = np.arange(kw)
    spread = np.zeros((kw, vw), np.float32)
    spread[v_col, v_col // HEAD_DIM * LANES + v_col % HEAD_DIM] = 1.0
    spread = jnp.asarray(spread)

    outs = []
    for b in range(batch):
        h = x[b]
        for l in range(depth):
            h = _ffn(h, row(g_ffn1[l]), bf(w1_gate[l]), bf(w1_up[l]), bf(w1_down[l]),
                     row(g_final), final_norm=False, tm=tiles["ffn_tokens"], dff_chunk=tiles["ffn_hidden"])

            wv = bf(jnp.dot(w_in[l][:, qw + kw:qw + 2 * kw], spread))
            shift = (LOG2E * math.sqrt(HEAD_DIM)) * jnp.max(jnp.abs(q_gain[l])) * jnp.max(jnp.abs(k_gain[l]))
            extra_lane = jnp.asarray((lane == HEAD_DIM).astype(np.float32)).reshape(1, LANES)
            q, kt, v, f, sf, sa = _inproj(
                h, row(g_mix[l]), bf(w_in[l]), wv, row(jnp.tile(q_gain[l], LANES // HEAD_DIM)),
                row(jnp.tile(k_gain[l], LANES // HEAD_DIM)), rope, head_mean,
                extra_lane, -shift * extra_lane, vone, widths=(qw, kw, fw, d), tm=tiles["tokens"])

            fm = _fft(f.reshape(n1, FFT_INNER, fw), cs1, tw_c, tw_s, cs2, mix,
                      nb2=tiles["fft_n2"], kb=tiles["fft_k1"])
            fm = fm.reshape(s, fw)

            later = (w_fourier_branch[l], w_attn_branch[l], w_out[l], w_cq[l], w_co[l], w_ck[l],
                     w_cv[l], w2_gate[l], w2_up[l], w2_down[l])
            o, (wfb, wab, wout, wcq, wco, wck, wcv, w2g, w2u, w2d) = lax.cond(
                shift <= MAX_SAFE_SHIFT,
                functools.partial(_attn, tq=tiles["attn_q"], q_per_step=tiles["attn_q_per_step"],
                                  tk=tiles["attn_kv"], online=False),
                functools.partial(_attn, tq=tiles["attn_online_q"], q_per_step=1,
                                  tk=tiles["attn_online_kv"], online=True),
                q, kt, v, later)

            h = _merge(h, fm, o, sf, sa, wfb, wab, wout, row(g_cross[l]), wcq, wco,
                       mem[b], row(g_mem[l]), wck, wcv,
                       tm=tiles["merge_tokens"], sub_tiles=tiles["merge_sub_tiles"])

            h = _ffn(h, row(g_ffn2[l]), w2g, w2u, w2d, row(g_final), final_norm=(l == depth - 1),
                     tm=tiles["ffn_tokens"], dff_chunk=tiles["ffn_hidden"])
        outs.append(h)
    return jnp.stack(outs, axis=0)
```

```python
import functools
import math

import numpy as np
import jax
import jax.numpy as jnp
from jax import lax
from jax.experimental import pallas as pl
from jax.experimental.pallas import tpu as pltpu

F32 = jnp.float32
BF16 = jnp.bfloat16

LANES = 128
MXU_WIDTH = 256
GRID_W = 64
HEAD_DIM = 64
N_KV_HEADS = 4
GQA_GROUP = 4
N_CROSS_HEADS = 4
FOURIER_GROUPS = 4
FOURIER_GROUP_DIM = 128
ROPE_AXIS_DIM = HEAD_DIM // 2
ROPE_HALF = ROPE_AXIS_DIM // 2
ROPE_THETA = 10000.0
RMS_EPS = 1e-6
FFT_INNER = 128
VMEM_LIMIT = 56 * 1024 * 1024
LOG2E = math.log2(math.e)
MAX_SAFE_SHIFT = 40.0


def _cparams(n_axes):
    return pltpu.CompilerParams(dimension_semantics=("arbitrary",) * n_axes,
                                vmem_limit_bytes=VMEM_LIMIT)


def _resident(shape):
    zeros = (0,) * len(shape)
    return pl.BlockSpec(shape, lambda *_: zeros, pipeline_mode=pl.Buffered(1))


def _dot(a, b):
    return jnp.dot(a, b, preferred_element_type=F32)


def _rms(x, g):
    y = x * lax.rsqrt(jnp.mean(x * x, axis=-1, keepdims=True) + RMS_EPS)
    return y * g


def _ffn_kernel(x_ref, g_ref, wg_ref, wu_ref, wd_ref, gfin_ref, *rest, final_norm, dff_chunk,
                side_blocks):
    o_ref = rest[-2] if side_blocks else rest[0]
    if side_blocks:
        @pl.when(pl.program_id(0) < side_blocks)
        def _():
            rest[-1][...] = rest[0][...].astype(BF16)
    x = x_ref[...]
    xn = _rms(x, g_ref[...]).astype(BF16)
    dff = wg_ref.shape[1]
    acc = None
    for lo in range(0, dff, dff_chunk):
        hi = min(lo + dff_chunk, dff)
        gate = _dot(xn, wg_ref[:, lo:hi])
        up = _dot(xn, wu_ref[:, lo:hi])
        act = (gate * jax.nn.sigmoid(gate) * up).astype(BF16)
        part = _dot(act, wd_ref[lo:hi, :])
        acc = part if acc is None else acc + part
    h = x + 0.5 * acc
    if final_norm:
        h = _rms(h, gfin_ref[...])
    o_ref[...] = h


def _ffn(x, g, wg, wu, wd, gfin, *, final_norm, tm, dff_chunk, side=None):
    s, d = x.shape
    dff = wg.shape[1]
    steps = s // tm
    row = pl.BlockSpec((tm, d), lambda i: (i, 0))
    in_specs = [row, _resident((1, d)), _resident((d, dff)), _resident((d, dff)),
                _resident((dff, d)), _resident((1, d))]
    out_specs, out_shape, args, nb = [row], [jax.ShapeDtypeStruct((s, d), F32)], [x, g, wg, wu, wd, gfin], 0
    if side is not None:
        nb = _row_blocks(side.shape[0], steps)
        spec = pl.BlockSpec((side.shape[0] // nb, side.shape[1]), lambda i: (jnp.minimum(i, nb - 1), 0))
        in_specs.append(spec)
        out_specs.append(spec)
        out_shape.append(jax.ShapeDtypeStruct(side.shape, BF16))
        args.append(side)
    outs = pl.pallas_call(
        functools.partial(_ffn_kernel, final_norm=final_norm, dff_chunk=dff_chunk, side_blocks=nb),
        grid=(steps,),
        in_specs=in_specs,
        out_specs=out_specs,
        out_shape=out_shape,
        compiler_params=_cparams(1),
        name="ffn_final" if final_norm else "ffn",
    )(*args)
    return outs if side is not None else outs[0]


def _head_norm_rope(x, gain, head_mean, cos, sin, first_half):
    ms = _dot((x * x).astype(BF16), head_mean)
    halves = []
    for c in range(x.shape[1] // LANES):
        sl = slice(c * LANES, (c + 1) * LANES)
        y = x[:, sl] * lax.rsqrt(ms[:, sl] + RMS_EPS) * gain
        partner = jnp.where(first_half,
                            pltpu.roll(y, LANES - ROPE_HALF, 1),
                            pltpu.roll(y, ROPE_HALF, 1))
        halves.append(y * cos + partner * sin)
    return halves


def _head_slots(xc, extra, low):
    return jnp.where(low, xc, extra), jnp.where(low, pltpu.roll(xc, HEAD_DIM, 1), extra)


def _token_table(row_ref, col_ref):
    col = col_ref[...]
    return jnp.concatenate([jnp.broadcast_to(row_ref[r:r + 1, :], col.shape) + col
                            for r in range(row_ref.shape[0])], axis=0)


def _inproj_kernel(h_ref, g_ref, win_ref, wv_ref, qg_ref, kg_ref, cosr_ref, sinr_ref, cosc_ref,
                   sinc_ref, hm_ref, qx_ref, kx_ref, vone_ref, q_ref, kt_ref, v_ref, f_ref, sf_ref,
                   sa_ref, *, widths):
    qw, kw, fw, gw = widths
    n = _rms(h_ref[...], g_ref[...]).astype(BF16)
    cos = _token_table(cosr_ref, cosc_ref)
    sin = _token_table(sinr_ref, sinc_ref)
    head_mean = hm_ref[...]
    lane = lax.broadcasted_iota(jnp.int32, cos.shape, 1)
    first_half = (lane % ROPE_AXIS_DIM) < ROPE_HALF
    low = lane < HEAD_DIM

    def proj(lo, width):
        return _dot(n, win_ref[:, lo:lo + width])

    q = proj(0, qw)
    pair = head_mean.shape[0]
    slot_lo = 0
    for c in range(qw // pair):
        for qc in _head_norm_rope(q[:, c * pair:(c + 1) * pair], qg_ref[...], head_mean, cos, sin,
                                  first_half):
            for slot in _head_slots(qc * (LOG2E / math.sqrt(HEAD_DIM)), qx_ref[...], low):
                q_ref[:, slot_lo:slot_lo + LANES] = slot.astype(BF16)
                slot_lo += LANES
    k = proj(qw, kw)
    k_slots = []
    for c in range(kw // pair):
        for kc in _head_norm_rope(k[:, c * pair:(c + 1) * pair], kg_ref[...], head_mean, cos, sin,
                                  first_half):
            k_slots.extend(_head_slots(kc, kx_ref[...], low))
    kt_ref[...] = jnp.concatenate(k_slots, axis=1).T.astype(BF16)
    v_ref[...] = (_dot(n, wv_ref[...]) + vone_ref[...]).astype(BF16)
    lo = qw + 2 * kw
    f_ref[...] = proj(lo, fw).astype(BF16)
    lo += fw
    sf_ref[...] = jax.nn.sigmoid(proj(lo, gw)).astype(sf_ref.dtype)
    lo += gw
    sa_ref[...] = jax.nn.sigmoid(proj(lo, gw)).astype(sa_ref.dtype)


def _inproj(h, g, win, wv, qg, kg, rope, head_mean, qx, kx, vone, *, widths, tm):
    s, d = h.shape
    qw, kw, fw, gw = widths
    vw = wv.shape[1]
    slots = LANES // HEAD_DIM
    assert tm % GRID_W == 0

    def rows(w):
        return pl.BlockSpec((tm, w), lambda i: (i, 0))

    lane_row = _resident((1, LANES))
    grid_rows = pl.BlockSpec((tm // GRID_W, LANES), lambda i: (i, 0))
    grid_cols = _resident((GRID_W, LANES))
    return pl.pallas_call(
        functools.partial(_inproj_kernel, widths=widths),
        grid=(s // tm,),
        in_specs=[rows(d), _resident((1, d)), _resident(win.shape), _resident(wv.shape), lane_row,
                  lane_row, grid_rows, grid_rows, grid_cols, grid_cols,
                  _resident(head_mean.shape), lane_row, lane_row, _resident((1, vw))],
        out_specs=[rows(qw * slots), pl.BlockSpec((kw * slots, tm), lambda i: (0, i)), rows(vw),
                   rows(fw), rows(gw), rows(gw)],
        out_shape=[jax.ShapeDtypeStruct((s, qw * slots), BF16),
                   jax.ShapeDtypeStruct((kw * slots, s), BF16),
                   jax.ShapeDtypeStruct((s, vw), BF16), jax.ShapeDtypeStruct((s, fw), BF16),
                   jax.ShapeDtypeStruct((s, gw), BF16), jax.ShapeDtypeStruct((s, gw), BF16)],
        compiler_params=_cparams(1),
        name="inproj",
    )(h, g, win, wv, qg, kg, *rope, head_mean, qx, kx, vone)


def _fft_stage1(step, x_ref, cs1_ref, a_ref):
    n1, nb2, width = x_ref.shape
    x = x_ref[...]
    tiles = [x[:, :, t * LANES:(t + 1) * LANES].reshape(n1, nb2 * LANES) for t in range(width // LANES)]
    res = [_dot(cs1_ref[...],
                jnp.concatenate([t[:, j * LANES:(j + 1) * LANES] for t in tiles], axis=1)).astype(a_ref.dtype)
           for j in range(nb2)]
    rows = pl.ds(pl.multiple_of(step * nb2, nb2), nb2)
    for t in range(width // LANES):
        side_by_side = jnp.concatenate([r[:, t * LANES:(t + 1) * LANES] for r in res], axis=1)
        a_ref[:, rows, t * LANES:(t + 1) * LANES] = side_by_side.reshape(2 * n1, nb2, LANES)


def _fft_stage2(blk, a_ref, tc_ref, ts_ref, rot2_ref, mix_ref, o_ref):
    n2, kb, width = o_ref.shape
    n1 = a_ref.shape[0] // 2
    gd = FOURIER_GROUP_DIM
    per_k1 = []
    for kk in range(kb):
        ac = a_ref[blk * kb + kk].astype(F32)
        as_ = a_ref[n1 + blk * kb + kk].astype(F32)
        tc = jnp.broadcast_to(tc_ref[0][:, kk:kk + 1], (n2, width))
        ts = jnp.broadcast_to(ts_ref[0][:, kk:kk + 1], (n2, width))
        b = jnp.concatenate([ac * tc - as_ * ts, as_ * tc + ac * ts], axis=0).astype(BF16)
        p = _dot(rot2_ref[...], b).astype(BF16)
        stacked = jnp.concatenate(
            [jnp.concatenate([p[:n2, g * gd:(g + 1) * gd], p[n2:, g * gd:(g + 1) * gd]], axis=1)
             for g in range(width // gd)], axis=0)
        per_k1.append(_dot(stacked, mix_ref[...]))
    for g in range(width // gd):
        side_by_side = jnp.concatenate([m[g * n2:(g + 1) * n2] for m in per_k1], axis=1)
        o_ref[:, :, g * gd:(g + 1) * gd] = side_by_side.reshape(n2, kb, gd).astype(o_ref.dtype)


def _fft_kernel(x_ref, cs1_ref, tc_ref, ts_ref, rot2_ref, mix_ref, o_ref, a_ref, *, stage1_steps):
    step = pl.program_id(0)

    @pl.when(step < stage1_steps)
    def _():
        _fft_stage1(step, x_ref, cs1_ref, a_ref)

    @pl.when(step >= stage1_steps)
    def _():
        _fft_stage2(step - stage1_steps, a_ref, tc_ref, ts_ref, rot2_ref, mix_ref, o_ref)


def _fft(x3, cs1, tc, ts, cs2, mix, *, nb2, kb):
    n1, n2, width = x3.shape
    stage1_steps = n2 // nb2
    stage2 = lambda j: jnp.maximum(j - stage1_steps, 0)
    return pl.pallas_call(
        functools.partial(_fft_kernel, stage1_steps=stage1_steps),
        grid=(stage1_steps + n1 // kb,),
        in_specs=[pl.BlockSpec((n1, nb2, width), lambda j: (0, jnp.minimum(j, stage1_steps - 1), 0)),
                  _resident(cs1.shape),
                  pl.BlockSpec((1, n2, kb), lambda j: (stage2(j), 0, 0)),
                  pl.BlockSpec((1, n2, kb), lambda j: (stage2(j), 0, 0)),
                  _resident(cs2.shape), _resident(mix.shape)],
        out_specs=pl.BlockSpec((n2, kb, width), lambda j: (0, stage2(j), 0)),
        out_shape=jax.ShapeDtypeStruct((n2, n1, width), BF16),
        scratch_shapes=[pltpu.VMEM((2 * n1, n2, width), BF16)],
        compiler_params=_cparams(1),
        name="fft",
    )(x3, cs1, tc, ts, cs2, mix)


def _dft_tables(s):
    n1 = s // FFT_INNER

    def cos_sin(n, scale):
        ang = 2.0 * np.pi * (np.outer(np.arange(n), np.arange(n)) % n) / n
        return np.cos(ang) * scale, np.sin(ang) * scale

    c1, s1 = cos_sin(n1, n1 ** -0.5)
    c2, s2 = cos_sin(FFT_INNER, FFT_INNER ** -0.5)
    cc, sc = cos_sin(FOURIER_GROUP_DIM, FOURIER_GROUP_DIM ** -0.5)
    as_bf16 = lambda a: jnp.asarray(a, dtype=F32).astype(BF16)
    return (as_bf16(np.concatenate([c1, s1], axis=0)), as_bf16(np.block([[c2, -s2], [s2, c2]])),
            as_bf16(np.concatenate([cc, -sc], axis=0)))


def _twiddles(s, kb):
    n1 = s // FFT_INNER
    ang = 2.0 * np.pi * (np.outer(np.arange(n1), np.arange(FFT_INNER)) % s) / s
    ang = ang.reshape(n1 // kb, kb, FFT_INNER).transpose(0, 2, 1)
    return jnp.asarray(np.cos(ang), dtype=F32), jnp.asarray(np.sin(ang), dtype=F32)


def _attn_kernel(q_ref, kt_ref, v_ref, *rest, tq, tk, online, side_blocks):
    n_side = len(side_blocks)
    o_ref = rest[n_side]
    step = pl.program_id(0) * pl.num_programs(1) + pl.program_id(1)
    for src, dst, nb in zip(rest[:n_side], rest[n_side + 1:], side_blocks):
        @pl.when(step < nb)
        def _(src=src, dst=dst):
            dst[...] = src[...].astype(dst.dtype)
    rows = GQA_GROUP * tq
    n_chunks = kt_ref.shape[1] // tk

    def query_block(b, carry):
        q_rows = pl.ds(pl.multiple_of(b * tq, tq), tq)
        qs = jnp.concatenate([q_ref[q_rows, g * LANES:(g + 1) * LANES] for g in range(GQA_GROUP)],
                             axis=0)
        acc0 = jnp.zeros((rows, LANES), F32)

        def scores(j):
            off = pl.multiple_of(j * tk, tk)
            return _dot(qs, kt_ref[:, pl.ds(off, tk)]), v_ref[pl.ds(off, tk), :]

        if online:
            def body(j, carry):
                m, acc = carry
                s, v = scores(j)
                m_new = jnp.maximum(m, jnp.max(s, axis=1, keepdims=True))
                p = jnp.exp2(s - m_new).astype(BF16)
                return m_new, acc * jnp.exp2(m - m_new) + _dot(p, v)

            m0 = jnp.full((rows, 1), -jnp.inf, F32)
            _, acc = lax.fori_loop(0, n_chunks, body, (m0, acc0))
        else:
            def body(j, acc):
                s, v = scores(j)
                return acc + _dot(jnp.exp2(s).astype(BF16), v)

            acc = lax.fori_loop(0, n_chunks, body, acc0)
        out = acc[:, :HEAD_DIM] / acc[:, HEAD_DIM:HEAD_DIM + 1]
        o_ref[q_rows, :] = jnp.concatenate([out[g * tq:(g + 1) * tq] for g in range(GQA_GROUP)],
                                           axis=1).astype(BF16)
        return carry

    lax.fori_loop(0, q_ref.shape[0] // tq, query_block, 0)


def _row_blocks(rows, max_blocks):
    for nb in range(min(max_blocks, rows // 16), 0, -1):
        if rows % nb == 0 and (rows // nb) % 16 == 0:
            return nb
    return 1


def _attn(q, kt, v, side, *, tq, q_per_step, tk, online):
    s = q.shape[0]
    gw = GQA_GROUP * HEAD_DIM
    tq_step = tq * q_per_step
    q_blocks = s // tq_step
    side_blocks = tuple(_row_blocks(w.shape[0], N_KV_HEADS * q_blocks) for w in side)
    side_specs = [pl.BlockSpec((w.shape[0] // nb, w.shape[1]),
                               lambda h, i, nb=nb: (jnp.minimum(h * q_blocks + i, nb - 1), 0))
                  for w, nb in zip(side, side_blocks)]
    outs = pl.pallas_call(
        functools.partial(_attn_kernel, tq=tq, tk=tk, online=online, side_blocks=side_blocks),
        grid=(N_KV_HEADS, q_blocks),
        in_specs=[pl.BlockSpec((tq_step, GQA_GROUP * LANES), lambda h, i: (i, h)),
                  pl.BlockSpec((LANES, s), lambda h, i: (h, 0)),
                  pl.BlockSpec((s, LANES), lambda h, i: (0, h))] + side_specs,
        out_specs=[pl.BlockSpec((tq_step, gw), lambda h, i: (i, h))] + side_specs,
        out_shape=[jax.ShapeDtypeStruct((s, N_KV_HEADS * gw), BF16)]
                  + [jax.ShapeDtypeStruct(w.shape, BF16) for w in side],
        compiler_params=_cparams(2),
        name="attn_online" if online else "attn",
    )(q, kt, v, *side)
    return outs[0], outs[1:]


def _memkv(mem_ref, g_ref, wk_ref, wv_ref, ktm_ref, vm_ref):
    mn = _rms(mem_ref[...], g_ref[...]).astype(BF16)
    kt = _dot(mn, wk_ref[...]).T
    v = _dot(mn, wv_ref[...])
    feat_row = lax.broadcasted_iota(jnp.int32, kt.shape, 0) // HEAD_DIM
    feat_col = lax.broadcasted_iota(jnp.int32, v.shape, 1) // HEAD_DIM
    for h in range(N_CROSS_HEADS):
        ktm_ref[h] = jnp.where(feat_row == h, kt, 0.0).astype(BF16)
        vm_ref[h] = jnp.where(feat_col == h, v, 0.0).astype(BF16)


def _merge_kernel(h_ref, fm_ref, o_ref, sf_ref, sa_ref, wfb_ref, wab_ref, wout_ref, gc_ref,
                  wcq_ref, wco_ref, mem_ref, gm_ref, wck_ref, wcv_ref, out_ref, ktm_ref, vm_ref,
                  *, sub_tiles):
    @pl.when(pl.program_id(0) == 0)
    def _():
        _memkv(mem_ref, gm_ref, wck_ref, wcv_ref, ktm_ref, vm_ref)

    sub = h_ref.shape[0] // sub_tiles
    tiles = [slice(r * sub, (r + 1) * sub) for r in range(sub_tiles)]
    y_f = [_dot(fm_ref[t, :], wfb_ref[...]) for t in tiles]
    y_a = [_dot(o_ref[t, :], wab_ref[...]) for t in tiles]
    merged = [(sf_ref[t, :] * f + sa_ref[t, :] * a).astype(BF16) for t, f, a in zip(tiles, y_f, y_a)]
    h = [h_ref[t, :] + _dot(m, wout_ref[...]) for t, m in zip(tiles, merged)]
    n = [_rms(x, gc_ref[...]).astype(BF16) for x in h]
    qc = [(_dot(x, wcq_ref[...]) * (1.0 / math.sqrt(HEAD_DIM))).astype(BF16) for x in n]
    oc = [None] * sub_tiles
    for hh in range(N_CROSS_HEADS):
        s = [_dot(x, ktm_ref[hh]) for x in qc]
        e = [jnp.exp(x - jnp.max(x, axis=1, keepdims=True)) for x in s]
        p = [(x / jnp.sum(x, axis=1, keepdims=True)).astype(BF16) for x in e]
        head = [_dot(x, vm_ref[hh]) for x in p]
        oc = [x if acc is None else acc + x for acc, x in zip(oc, head)]
    for t, x, o in zip(tiles, h, oc):
        out_ref[t, :] = x + _dot(o.astype(BF16), wco_ref[...])


def _merge(h, fm, o, sf, sa, wfb, wab, wout, gc, wcq, wco, mem, gm, wck, wcv, *, tm, sub_tiles):
    s, d = h.shape
    m, cw = mem.shape[0], wck.shape[1]

    def rows(w):
        return pl.BlockSpec((tm, w), lambda i: (i, 0))

    weights = [wfb, wab, wout, gc, wcq, wco, mem, gm, wck, wcv]
    return pl.pallas_call(
        functools.partial(_merge_kernel, sub_tiles=sub_tiles),
        grid=(s // tm,),
        in_specs=[rows(d), rows(fm.shape[1]), rows(o.shape[1]), rows(d), rows(d)]
                 + [_resident(w.shape) for w in weights],
        out_specs=rows(d),
        out_shape=jax.ShapeDtypeStruct((s, d), F32),
        scratch_shapes=[pltpu.VMEM((N_CROSS_HEADS, cw, m), BF16),
                        pltpu.VMEM((N_CROSS_HEADS, m, cw), BF16)],
        compiler_params=_cparams(1),
        name="merge",
    )(h, fm, o, sf, sa, *weights)


def _rope_tables(s):
    f32 = np.float32
    lane = np.arange(LANES)
    inv = f32(1.0) / (f32(ROPE_THETA) ** (np.arange(0, ROPE_AXIS_DIM, 2, dtype=f32) / f32(ROPE_AXIS_DIM)))
    inv_lane = inv[lane % ROPE_HALF]
    row_lane = (lane % HEAD_DIM) < ROPE_AXIS_DIM
    sign = np.where((lane % ROPE_AXIS_DIM) < ROPE_HALF, f32(-1.0), f32(1.0))
    ang_r = np.arange(s // GRID_W, dtype=f32)[:, None] * inv_lane[None, :]
    ang_c = np.arange(GRID_W, dtype=f32)[:, None] * inv_lane[None, :]
    tables = (np.where(row_lane, np.cos(ang_r), 0.0), np.where(row_lane, np.sin(ang_r) * sign, 0.0),
              np.where(row_lane, 0.0, np.cos(ang_c)), np.where(row_lane, 0.0, np.sin(ang_c) * sign))
    return tuple(jnp.asarray(t, dtype=F32) for t in tables)


def _pick_tile(s, want):
    t = min(want, s)
    while s % t:
        t //= 2
    return t


def _tiles(s):
    tm = _pick_tile(s, 512)
    return dict(
        tokens=tm,
        ffn_tokens=_pick_tile(s, 2 * tm),
        ffn_hidden=3 * MXU_WIDTH,
        merge_tokens=_pick_tile(s, 2 * tm),
        merge_sub_tiles=4,
        fft_n2=16,
        fft_k1=min(16, s // FFT_INNER),
        attn_q=_pick_tile(s, 512),
        attn_q_per_step=max(1, min(2, s // _pick_tile(s, 512))),
        attn_kv=_pick_tile(s, 4096),
        attn_online_q=_pick_tile(s, 128),
        attn_online_kv=_pick_tile(s, 512),
    )


def kernel(x, mem, g_ffn1, w1_gate, w1_up, w1_down, g_mix, w_in, q_gain, k_gain, w_fourier_branch, w_attn_branch, w_out, g_cross, g_mem, w_cq, w_ck, w_cv, w_co, g_ffn2, w2_gate, w2_up, w2_down, g_final):
    batch, s, d = x.shape
    depth = w_in.shape[0]
    qw = w_attn_branch.shape[1]
    kw = N_KV_HEADS * HEAD_DIM
    fw = w_fourier_branch.shape[1]
    vw = N_KV_HEADS * LANES
    assert s % (FFT_INNER * 8) == 0 and s % GRID_W == 0
    assert qw == N_KV_HEADS * GQA_GROUP * HEAD_DIM and fw == FOURIER_GROUPS * FOURIER_GROUP_DIM
    n1 = s // FFT_INNER
    tiles = _tiles(s)
    row = lambda g: g.reshape(1, -1).astype(F32)
    bf = lambda w: w.astype(BF16)

    rope = _rope_tables(s)
    cs1, cs2, mix = _dft_tables(s)
    tw_c, tw_s = _twiddles(s, tiles["fft_k1"])
    lane = np.arange(LANES)
    mxu_lane = np.arange(MXU_WIDTH)
    head_mean = jnp.asarray((mxu_lane[:, None] // HEAD_DIM == mxu_lane[None, :] // HEAD_DIM) / HEAD_DIM,
                            dtype=F32).astype(BF16)
    vone = jnp.asarray((np.arange(vw) % LANES == HEAD_DIM).astype(np.float32)).reshape(1, vw)
    v_col = np.arange(kw)
    spread = np.zeros((kw, vw), np.float32)
    spread[v_col, v_col // HEAD_DIM * LANES + v_col % HEAD_DIM] = 1.0
    spread = jnp.asarray(spread)

    outs = []
    for b in range(batch):
        h = x[b]
        for l in range(depth):
            h, w_in_bf = _ffn(h, row(g_ffn1[l]), bf(w1_gate[l]), bf(w1_up[l]), bf(w1_down[l]),
                              row(g_final), final_norm=False, tm=tiles["ffn_tokens"],
                              dff_chunk=tiles["ffn_hidden"], side=w_in[l])

            wv = bf(jnp.dot(w_in[l][:, qw + kw:qw + 2 * kw], spread))
            shift = (LOG2E * math.sqrt(HEAD_DIM)) * jnp.max(jnp.abs(q_gain[l])) * jnp.max(jnp.abs(k_gain[l]))
            extra_lane = jnp.asarray((lane == HEAD_DIM).astype(np.float32)).reshape(1, LANES)
            q, kt, v, f, sf, sa = _inproj(
                h, row(g_mix[l]), w_in_bf, wv, row(jnp.tile(q_gain[l], LANES // HEAD_DIM)),
                row(jnp.tile(k_gain[l], LANES // HEAD_DIM)), rope, head_mean,
                extra_lane, -shift * extra_lane, vone, widths=(qw, kw, fw, d), tm=tiles["tokens"])

            fm = _fft(f.reshape(n1, FFT_INNER, fw), cs1, tw_c, tw_s, cs2, mix,
                      nb2=tiles["fft_n2"], kb=tiles["fft_k1"])
            fm = fm.reshape(s, fw)

            later = (w_fourier_branch[l], w_attn_branch[l], w_out[l], w_cq[l], w_co[l], w_ck[l],
                     w_cv[l], w2_gate[l], w2_up[l], w2_down[l])
            o, (wfb, wab, wout, wcq, wco, wck, wcv, w2g, w2u, w2d) = lax.cond(
                shift <= MAX_SAFE_SHIFT,
                functools.partial(_attn, tq=tiles["attn_q"], q_per_step=tiles["attn_q_per_step"],
                                  tk=tiles["attn_kv"], online=False),
                functools.partial(_attn, tq=tiles["attn_online_q"], q_per_step=1,
                                  tk=tiles["attn_online_kv"], online=True),
                q, kt, v, later)

            h = _merge(h, fm, o, sf, sa, wfb, wab, wout, row(g_cross[l]), wcq, wco,
                       mem[b], row(g_mem[l]), wck, wcv,
                       tm=tiles["merge_tokens"], sub_tiles=tiles["merge_sub_tiles"])

            h = _ffn(h, row(g_ffn2[l]), w2g, w2u, w2d, row(g_final), final_norm=(l == depth - 1),
                     tm=tiles["ffn_tokens"], dff_chunk=tiles["ffn_hidden"])
        outs.append(h)
    return jnp.stack(outs, axis=0)
```
